```python
import math
import jax, jax.numpy as jnp
from jax import lax
import numpy as np

D_MODEL = 1024
BATCH = 2
SEQ = 8192
DEPTH = 4
DEC_BATCH = 32
DEC_SEQ = 8
PAST_LEN = 8192
PAGE_SIZE = 128

H_A = 8
DH_A = 64
W_A = H_A * DH_A
MOBA_BLOCK = 256
MOBA_TOPK = 3
MOBA_QB = 128
W_B = 512
CONV_B = 3
H_C = 8
DK_C = 64
DV_C = 64
W_C = H_C * DV_C
CONV_C = 4
DN_CHUNK = 64
D_FF = 2816
CONV_F = 3
REL_BUCKETS = 32
REL_MAX_DIST = 2048
N_BRANCH = 3
EPS = 1e-6
NEG = -1e30

kernel_name = 'hybrid_moba_shortconv_gdn_decoder_step'


def rmsnorm(x, g):
    xf = x.astype(jnp.float32)
    y = xf * lax.rsqrt(jnp.mean(xf * xf, axis=-1, keepdims=True) + EPS)
    return (y * g.astype(jnp.float32)).astype(x.dtype)


def l2norm(x):
    return x * lax.rsqrt(jnp.sum(x * x, axis=-1, keepdims=True) + EPS)


def split_in(z):
    sizes = (W_A, W_A, W_A, W_B, W_B, W_B, 3 * W_C, W_C, H_C, H_C, N_BRANCH * D_MODEL)
    cuts, acc = [], 0
    for s in sizes[:-1]:
        acc += s
        cuts.append(acc)
    return jnp.split(z, cuts, axis=-1)


def causal_conv(x, state, w):
    T = x.shape[1]
    xp = jnp.concatenate([state.astype(x.dtype), x], axis=1)
    y = xp[:, 0:T] * w[0]
    for i in range(1, w.shape[0]):
        y = y + xp[:, i:i + T] * w[i]
    return y, xp[:, T:]


def t5_bucket(rel):
    n = jnp.maximum(rel, 0)
    max_exact = REL_BUCKETS // 2
    nf = jnp.maximum(n, 1).astype(jnp.float32)
    large = max_exact + (jnp.log(nf / max_exact) / math.log(REL_MAX_DIST / max_exact)
                         * (REL_BUCKETS - max_exact)).astype(jnp.int32)
    large = jnp.minimum(large, REL_BUCKETS - 1)
    return jnp.where(n < max_exact, n, large)


def moba_attend(q, k, v, q_pos, rel_bias, qb):
    B, T, H, DH = q.shape
    nb = k.shape[1] // MOBA_BLOCK
    kb = k.reshape(B, nb, MOBA_BLOCK, H, DH)
    vb = v.reshape(B, nb, MOBA_BLOCK, H, DH)
    kmean = jnp.mean(kb.astype(jnp.float32), axis=2)
    top = min(MOBA_TOPK, nb)
    nq = T // qb
    qs = q.reshape(B, nq, qb, H, DH).transpose(1, 0, 3, 2, 4)
    ps = q_pos.reshape(nq, qb)
    bi = jnp.arange(B)[:, None, None, None]
    hi = jnp.arange(H)[None, :, None, None]
    blk_ids = jnp.arange(nb, dtype=jnp.int32)
    offs = jnp.arange(MOBA_BLOCK, dtype=jnp.int32)

    def attend_block(args):
        qc, pc = args
        own = pc // MOBA_BLOCK
        gate = jnp.einsum('bhqd,bnhd->bhqn', qc.astype(jnp.float32), kmean)
        gate = jnp.where(blk_ids[None, None, None, :] < own[None, None, :, None], gate, NEG)
        _, sel_top = lax.top_k(gate, top)
        own_b = jnp.broadcast_to(own[None, None, :, None], (B, H, qb, 1))
        sel = jnp.concatenate([sel_top.astype(jnp.int32), own_b], axis=-1)
        valid = jnp.concatenate([sel_top < own_b, jnp.ones_like(own_b, dtype=bool)], axis=-1)
        kg = kb[bi, sel, :, hi]
        vg = vb[bi, sel, :, hi]
        kpos = sel[..., None] * MOBA_BLOCK + offs
        rel = pc[None, None, :, None, None] - kpos
        bias = rel_bias[t5_bucket(rel), hi[..., None]]
        s = jnp.einsum('bhqd,bhqskd->bhqsk', qc, kg, preferred_element_type=jnp.float32) + bias
        s = jnp.where(valid[..., None] & (rel >= 0), s, NEG).reshape(B, H, qb, -1)
        p = jax.nn.softmax(s, axis=-1).astype(v.dtype)
        return jnp.einsum('bhqn,bhqnd->bhqd', p, vg.reshape(B, H, qb, -1, DH))

    o = lax.map(attend_block, (qs, ps))
    return o.transpose(1, 0, 3, 2, 4).reshape(B, T, H * DH)


def gated_delta(q, k, v, g, beta, s0):
    B, T, H, DK = q.shape
    C = min(DN_CHUNK, T)
    pad = (-T) % C
    if pad:
        pw = ((0, 0), (0, pad), (0, 0), (0, 0))
        q, k, v = jnp.pad(q, pw), jnp.pad(k, pw), jnp.pad(v, pw)
        g, beta = jnp.pad(g, pw[:3]), jnp.pad(beta, pw[:3])
    N = (T + pad) // C

    def chunks(a):
        a = a.reshape((B, N, C) + a.shape[2:])
        return jnp.moveaxis(a, (1, 3), (0, 2))

    qc, kc, vc = chunks(q) * (DK ** -0.5), chunks(k), chunks(v)
    gcum = jnp.cumsum(chunks(g), axis=-1)
    bc = chunks(beta)[..., None]
    tril = jnp.tril(jnp.ones((C, C), dtype=bool))
    strict = jnp.tril(jnp.ones((C, C), dtype=bool), -1)
    diff = gcum[..., :, None] - gcum[..., None, :]
    decay = jnp.where(tril, jnp.exp(jnp.where(tril, diff, 0.0)), 0.0)
    kbeta = kc * bc
    a = jnp.where(strict, jnp.einsum('nbhid,nbhjd->nbhij', kbeta, kc) * decay, 0.0)
    eye = jnp.eye(C, dtype=a.dtype)
    t_inv = lax.linalg.triangular_solve(a + eye, jnp.broadcast_to(eye, a.shape),
                                        left_side=True, lower=True, unit_diagonal=True)
    u = t_inv @ (vc * bc)
    w = t_inv @ (kbeta * jnp.exp(gcum)[..., None])
    intra = jnp.einsum('nbhid,nbhjd->nbhij', qc, kc) * decay
    glast = gcum[..., -1:]
    q_dec = qc * jnp.exp(gcum)[..., None]
    k_dec = kc * jnp.exp(glast - gcum)[..., None]

    def step(s, inp):
        qd, kd, u_n, w_n, ai, gl = inp
        v_new = u_n - jnp.einsum('bhck,bhkv->bhcv', w_n, s)
        o = jnp.einsum('bhck,bhkv->bhcv', qd, s) + jnp.einsum('bhij,bhjv->bhiv', ai, v_new)
        s = s * jnp.exp(gl)[..., None] + jnp.einsum('bhck,bhcv->bhkv', kd, v_new)
        return s, o

    s_fin, o = lax.scan(step, s0, (q_dec, k_dec, u, w, intra, glast))
    o = jnp.moveaxis(o, (0, 2), (1, 3)).reshape(B, N * C, H, v.shape[-1])[:, :T]
    return o, s_fin


def trunk_layer(x, c, conv_b0, conv_dn0, s0, conv_f0, past_k, past_v, qb, rel_bias, p):
    B, T, _ = x.shape
    mod = (jax.nn.silu(c) @ p['w_ada'] + p['b_ada'])[:, None, :]
    sh1, sc1, g1, sh2, sc2, g2 = jnp.split(mod, 6, axis=-1)
    h = rmsnorm(x, p['norm1_g']) * (1 + sc1) + sh1
    qa, ka, va, hb, bg, cg, qkv_c, og_c, b_c, a_c, mg = split_in(h @ p['w_in'])

    qa = rmsnorm(qa.reshape(B, T, H_A, DH_A), p['qn_g']) * (DH_A ** -0.5)
    ka = rmsnorm(ka.reshape(B, T, H_A, DH_A), p['kn_g'])
    va = va.reshape(B, T, H_A, DH_A)
    pos0 = 0 if past_k is None else past_k.shape[1]
    pad = (-(pos0 + T)) % MOBA_BLOCK
    zpad = jnp.zeros((B, pad, H_A, DH_A), ka.dtype)
    if past_k is None:
        k_all = jnp.concatenate([ka, zpad], axis=1)
        v_all = jnp.concatenate([va, zpad], axis=1)
    else:
        k_all = jnp.concatenate([past_k.astype(ka.dtype), ka, zpad], axis=1)
        v_all = jnp.concatenate([past_v.astype(va.dtype), va, zpad], axis=1)
    q_pos = pos0 + jnp.arange(T, dtype=jnp.int32)
    ya = moba_attend(qa, k_all, v_all, q_pos, rel_bias, qb) @ p['w_oa']

    uc, conv_b_new = causal_conv(cg * hb, conv_b0, p['conv_b_w'])
    yb = (bg * uc) @ p['w_ob']

    qkv, conv_dn_new = causal_conv(qkv_c, conv_dn0, p['dn_conv_w'])
    qkv = jax.nn.silu(qkv).astype(jnp.float32)
    qc, kc, vc = jnp.split(qkv, 3, axis=-1)
    qc = l2norm(qc.reshape(B, T, H_C, DK_C))
    kc = l2norm(kc.reshape(B, T, H_C, DK_C))
    vc = vc.reshape(B, T, H_C, DV_C)
    beta = jax.nn.sigmoid(b_c.astype(jnp.float32))
    g = -jnp.exp(p['dn_a_log'].astype(jnp.float32)) * jax.nn.softplus(
        a_c.astype(jnp.float32) + p['dn_dt_bias'].astype(jnp.float32))
    oc, s_new = gated_delta(qc, kc, vc, g, beta, s0.astype(jnp.float32))
    oc = rmsnorm(oc, p['dn_norm_g']) * jax.nn.silu(og_c.reshape(B, T, H_C, DV_C).astype(jnp.float32))
    yc = oc.reshape(B, T, W_C).astype(x.dtype) @ p['w_oc']

    ga, gb, gc = jnp.split(jax.nn.sigmoid(mg), 3, axis=-1)
    x = x + g1 * ((ga * ya + gb * yb + gc * yc) @ p['w_o'])

    h2 = rmsnorm(x, p['norm2_g']) * (1 + sc2) + sh2
    hg, conv_f_new = causal_conv(h2 @ p['w_gate'], conv_f0, p['ffn_conv_w'])
    x = x + g2 * ((jax.nn.silu(hg) * (h2 @ p['w_up'])) @ p['w_down'])
    return x, ka, va, conv_b_new, conv_dn_new, s_new, conv_f_new


def setup_inputs(seed: int = 0) -> dict:
    key = jax.random.key(seed)
    ks = jax.random.split(key, 40)
    n_pages = PAST_LEN // PAGE_SIZE
    n_used = DEC_BATCH * n_pages
    n_pool = n_used + n_used // 4
    f32 = jnp.float32

    def nrm(k, shape, s):
        return jax.random.normal(k, shape, f32) * s

    def gain(k, shape):
        return 1.0 + 0.1 * jax.random.normal(k, shape, f32)

    n_in = 3 * W_A + 3 * W_B + 4 * W_C + 2 * H_C + N_BRANCH * D_MODEL
    page_table = jax.random.permutation(ks[0], n_pool)[:n_used].reshape(DEC_BATCH, n_pages).astype(jnp.int32)
    dt = jnp.exp(jax.random.uniform(ks[1], (DEPTH, H_C), f32, math.log(1e-3), math.log(1e-1)))
    return {
        'x_prompt': nrm(ks[2], (BATCH, SEQ, D_MODEL), 1.0),
        'x_sample': nrm(ks[3], (DEC_BATCH, DEC_SEQ, D_MODEL), 1.0),
        'cache_k': nrm(ks[4], (DEPTH, n_pool, PAGE_SIZE, H_A, DH_A), 1.0),
        'cache_v': nrm(ks[5], (DEPTH, n_pool, PAGE_SIZE, H_A, DH_A), 1.0),
        'state_conv_b': nrm(ks[6], (DEPTH, DEC_BATCH, CONV_B - 1, W_B), 1.0),
        'state_conv_dn': nrm(ks[7], (DEPTH, DEC_BATCH, CONV_C - 1, 3 * W_C), 1.0),
        'state_dn': nrm(ks[8], (DEPTH, DEC_BATCH, H_C, DK_C, DV_C), DK_C ** -0.5),
        'state_conv_ffn': nrm(ks[9], (DEPTH, DEC_BATCH, CONV_F - 1, D_FF), 1.0),
        'page_table': page_table,
        'c_prompt': nrm(ks[10], (BATCH, D_MODEL), 1.0),
        'c_sample': nrm(ks[11], (DEC_BATCH, D_MODEL), 1.0),
        'rel_bias': nrm(ks[12], (REL_BUCKETS, H_A), 0.5),
        'w_ada': nrm(ks[13], (DEPTH, D_MODEL, 6 * D_MODEL), D_MODEL ** -0.5),
        'b_ada': nrm(ks[14], (DEPTH, 6 * D_MODEL), 0.02),
        'norm1_g': gain(ks[15], (DEPTH, D_MODEL)),
        'norm2_g': gain(ks[16], (DEPTH, D_MODEL)),
        'w_in': nrm(ks[17], (DEPTH, D_MODEL, n_in), D_MODEL ** -0.5),
        'qn_g': gain(ks[18], (DEPTH, DH_A)),
        'kn_g': gain(ks[19], (DEPTH, DH_A)),
        'conv_b_w': nrm(ks[20], (DEPTH, CONV_B, W_B), CONV_B ** -0.5),
        'dn_conv_w': nrm(ks[21], (DEPTH, CONV_C, 3 * W_C), CONV_C ** -0.5),
        'dn_a_log': jnp.log(jax.random.uniform(ks[22], (DEPTH, H_C), f32, 1.0, 16.0)),
        'dn_dt_bias': dt + jnp.log(-jnp.expm1(-dt)),
        'dn_norm_g': gain(ks[23], (DEPTH, DV_C)),
        'w_oa': nrm(ks[24], (DEPTH, W_A, D_MODEL), W_A ** -0.5),
        'w_ob': nrm(ks[25], (DEPTH, W_B, D_MODEL), W_B ** -0.5),
        'w_oc': nrm(ks[26], (DEPTH, W_C, D_MODEL), W_C ** -0.5),
        'w_o': nrm(ks[27], (DEPTH, D_MODEL, D_MODEL), D_MODEL ** -0.5),
        'w_gate': nrm(ks[28], (DEPTH, D_MODEL, D_FF), D_MODEL ** -0.5),
        'w_up': nrm(ks[29], (DEPTH, D_MODEL, D_FF), D_MODEL ** -0.5),
        'ffn_conv_w': nrm(ks[30], (DEPTH, CONV_F, D_FF), CONV_F ** -0.5),
        'w_down': nrm(ks[31], (DEPTH, D_FF, D_MODEL), D_FF ** -0.5),
    }


def reference(x_prompt, x_sample, cache_k, cache_v, state_conv_b, state_conv_dn, state_dn,
              state_conv_ffn, page_table, c_prompt, c_sample, rel_bias, w_ada, b_ada, norm1_g,
              norm2_g, w_in, qn_g, kn_g, conv_b_w, dn_conv_w, dn_a_log, dn_dt_bias, dn_norm_g,
              w_oa, w_ob, w_oc, w_o, w_gate, w_up, ffn_conv_w, w_down):
    bp, tp = x_prompt.shape[0], x_prompt.shape[1]
    bs, ts = x_sample.shape[0], x_sample.shape[1]
    n_pages = page_table.shape[1]
    past_len = n_pages * cache_k.shape[2]
    dt = x_prompt.dtype
    xp, xs = x_prompt, x_sample
    kp, vp, ks_, vs_ = [], [], [], []
    cbp, cbs, cdp, cds, sdp, sds, cfp, cfs = [], [], [], [], [], [], [], []
    qb_p = min(MOBA_QB, tp)
    for l in range(DEPTH):
        p = {'w_ada': w_ada[l], 'b_ada': b_ada[l], 'norm1_g': norm1_g[l], 'norm2_g': norm2_g[l],
             'w_in': w_in[l], 'qn_g': qn_g[l], 'kn_g': kn_g[l], 'conv_b_w': conv_b_w[l],
             'dn_conv_w': dn_conv_w[l], 'dn_a_log': dn_a_log[l], 'dn_dt_bias': dn_dt_bias[l],
             'dn_norm_g': dn_norm_g[l], 'w_oa': w_oa[l], 'w_ob': w_ob[l], 'w_oc': w_oc[l],
             'w_o': w_o[l], 'w_gate': w_gate[l], 'w_up': w_up[l], 'ffn_conv_w': ffn_conv_w[l],
             'w_down': w_down[l]}
        xp, k1, v1, cb1, cd1, s1, cf1 = trunk_layer(
            xp, c_prompt,
            jnp.zeros((bp, CONV_B - 1, W_B), dt),
            jnp.zeros((bp, CONV_C - 1, 3 * W_C), dt),
            jnp.zeros((bp, H_C, DK_C, DV_C), jnp.float32),
            jnp.zeros((bp, CONV_F - 1, D_FF), dt),
            None, None, qb_p, rel_bias, p)
        pk = cache_k[l][page_table].reshape(bs, past_len, H_A, DH_A)
        pv = cache_v[l][page_table].reshape(bs, past_len, H_A, DH_A)
        xs, k2, v2, cb2, cd2, s2, cf2 = trunk_layer(
            xs, c_sample, state_conv_b[l], state_conv_dn[l], state_dn[l], state_conv_ffn[l],
            pk, pv, 1, rel_bias, p)
        kp.append(k1); vp.append(v1); ks_.append(k2); vs_.append(v2)
        cbp.append(cb1); cbs.append(cb2); cdp.append(cd1); cds.append(cd2)
        sdp.append(s1); sds.append(s2); cfp.append(cf1); cfs.append(cf2)
    return (xp, xs, jnp.stack(kp), jnp.stack(vp), jnp.stack(ks_), jnp.stack(vs_),
            jnp.stack(cbp), jnp.stack(cbs), jnp.stack(cdp), jnp.stack(cds),
            jnp.stack(sdp), jnp.stack(sds), jnp.stack(cfp), jnp.stack(cfs))
```

```python
import functools
import math

import jax
import jax.numpy as jnp
from jax import lax
from jax.experimental import pallas as pl
from jax.experimental.pallas import tpu as pltpu

F32 = jnp.float32
BF16 = jnp.bfloat16

N_HEAD = 8
D_HEAD = 64
W_BR = N_HEAD * D_HEAD
MOBA_BLOCK = 256
MOBA_TOPK = 3
CONV_B = 3
CONV_C = 4
CONV_F = 3
DN_CHUNK = 64
REL_BUCKETS = 32
REL_MAX_DIST = 2048
N_BRANCH = 3
EPS = 1e-6
NEG = -1e30

TM = MOBA_BLOCK
SUBLANE = 8
VMEM_LIMIT = 56 * 1024 * 1024
HIGHEST = lax.Precision.HIGHEST

_NT = (((1,), (1,)), ((), ()))
_TN = (((0,), (0,)), ((), ()))


def _const_spec(shape):
    nd = len(shape)
    return pl.BlockSpec(shape, lambda *_: (0,) * nd, pipeline_mode=pl.Buffered(1))


def _cparams(n_grid):
    return pltpu.CompilerParams(dimension_semantics=("arbitrary",) * n_grid,
                                vmem_limit_bytes=VMEM_LIMIT)


def _sigmoid(x):
    return 1.0 / (1.0 + jnp.exp(-x))


def _silu(x):
    return x * _sigmoid(x)


def _softplus(x):
    return jnp.maximum(x, 0.0) + jnp.log1p(jnp.exp(-jnp.abs(x)))


def _split_dot(a, b_bf16):
    hi = a.astype(BF16)
    lo = (a - hi.astype(F32)).astype(BF16)
    return (jnp.dot(hi, b_bf16, preferred_element_type=F32)
            + jnp.dot(lo, b_bf16, preferred_element_type=F32))


def _dot(a, b):
    if a.ndim == 3:
        return lax.dot_general(a, b, (((2,), (1,)), ((0,), (0,))), preferred_element_type=F32)
    return jnp.dot(a, b, preferred_element_type=F32)


def _mm3(a, b):
    ah = a.astype(BF16)
    al = (a - ah.astype(F32)).astype(BF16)
    bh = b.astype(BF16)
    bl = (b - bh.astype(F32)).astype(BF16)
    return _dot(ah, bh) + _dot(ah, bl) + _dot(al, bh)


def _group_sumsq(y, gsum_bf16):
    return _split_dot(y * y, gsum_bf16)


def _mod_norm(x, gain, sc, sh):
    ms = jnp.mean(x * x, axis=-1, keepdims=True)
    return (x * lax.rsqrt(ms + EPS) * gain) * (1.0 + sc) + sh


def _causal_conv(xp_ref, cur, w_ref, halo_ref, first, *, width, halo, stride, rows):
    @pl.when(first)
    def _():
        xp_ref[0:halo, :] = halo_ref[0]

    xp_ref[halo:halo + rows, :] = cur
    off = halo - (width - 1) * stride
    y = xp_ref[off:off + rows, :] * w_ref[0:1, :]
    for i in range(1, width):
        off = halo - (width - 1 - i) * stride
        y = y + xp_ref[off:off + rows, :] * w_ref[i:i + 1, :]
    tail = xp_ref[rows:rows + halo, :]
    xp_ref[0:halo, :] = tail
    return y, tail


def _ada_kernel(c_ref, w_ref, b_ref, o_ref):
    c = _silu(c_ref[...]).astype(BF16)
    o_ref[0] = jnp.dot(c, w_ref[0].astype(BF16), preferred_element_type=F32) + b_ref[0]


def _ada_call(c_all, w_ada, b_ada):
    depth, d_model, n_mod = w_ada.shape
    rows = c_all.shape[0]
    bn = 1536
    return pl.pallas_call(
        _ada_kernel,
        grid=(depth, n_mod // bn),
        in_specs=[pl.BlockSpec((rows, d_model), lambda l, j: (0, 0)),
                  pl.BlockSpec((1, d_model, bn), lambda l, j: (l, 0, j)),
                  pl.BlockSpec((1, 1, bn), lambda l, j: (l, 0, j))],
        out_specs=pl.BlockSpec((1, rows, bn), lambda l, j: (l, 0, j)),
        out_shape=jax.ShapeDtypeStruct((depth, rows, n_mod), F32),
        compiler_params=_cparams(2),
    )(c_all, w_ada, b_ada.reshape(depth, 1, n_mod))


def _in_kernel(*refs, prompt, rows, tiles_per_seq, stride, halo_b, halo_c):
    (x_ref, sc_ref, sh_ref, g1_ref, wa_ref, wb_ref, wc_ref, wba_ref, wbat_ref, qn_ref, kn_ref,
     cbw_ref, cdw_ref, alog_ref, dtb_ref, alogc_ref, dtbc_ref, gsum_ref, hb_ref, hc_ref) = refs[:20]
    outs = refs[20:-2]
    xpb_ref, xpc_ref = refs[-2:]
    if prompt:
        (q_ref, k_ref, v_ref, yb_ref, tailb_ref, qc_ref, kc_ref, vc_ref, og_ref, beta_ref,
         gcol_ref, tailc_ref, kbf_ref, vt_ref, kmean_ref, grow_ref) = outs
    else:
        (q_ref, k_ref, v_ref, yb_ref, tailb_ref, qc_ref, kc_ref, vc_ref, og_ref, beta_ref,
         gcol_ref, tailc_ref) = outs

    first = (pl.program_id(0) % tiles_per_seq) == 0
    gsum = gsum_ref[...]
    h = _mod_norm(x_ref[...], g1_ref[...], sc_ref[0], sh_ref[0])
    hb16 = h.astype(BF16)

    za = jnp.dot(hb16, wa_ref[...], preferred_element_type=F32)
    qa, ka, va = za[:, :W_BR], za[:, W_BR:2 * W_BR], za[:, 2 * W_BR:]
    q = (qa * lax.rsqrt(_group_sumsq(qa, gsum) * (1.0 / D_HEAD) + EPS) * qn_ref[...]) * (D_HEAD ** -0.5)
    k = ka * lax.rsqrt(_group_sumsq(ka, gsum) * (1.0 / D_HEAD) + EPS) * kn_ref[...]
    q_ref[...] = q
    k_ref[...] = k
    v_ref[...] = va
    if prompt:
        kbf_ref[0] = k.astype(BF16)
        vt_ref[0] = va.T.astype(BF16)
        kmean_ref[0] = jnp.sum(k, axis=0, keepdims=True) * (1.0 / MOBA_BLOCK)

    zb = jnp.dot(hb16, wb_ref[...], preferred_element_type=F32)
    hb, bg, cg = zb[:, :W_BR], zb[:, W_BR:2 * W_BR], zb[:, 2 * W_BR:]
    uc, tail_b = _causal_conv(xpb_ref, cg * hb, cbw_ref, hb_ref, first,
                              width=CONV_B, halo=halo_b, stride=stride, rows=rows)
    yb_ref[...] = bg * uc
    tailb_ref[0] = tail_b

    zc = jnp.dot(hb16, wc_ref[...], preferred_element_type=F32)
    qkv, tail_c = _causal_conv(xpc_ref, zc[:, :3 * W_BR], cdw_ref, hc_ref, first,
                               width=CONV_C, halo=halo_c, stride=stride, rows=rows)
    tailc_ref[0] = tail_c
    og_ref[...] = zc[:, 3 * W_BR:]
    qkv = _silu(qkv)
    qc, kc, vc = qkv[:, :W_BR], qkv[:, W_BR:2 * W_BR], qkv[:, 2 * W_BR:]
    qc_ref[...] = qc * lax.rsqrt(_group_sumsq(qc, gsum) + EPS)
    kc_ref[...] = kc * lax.rsqrt(_group_sumsq(kc, gsum) + EPS)
    vc_ref[...] = vc
    zba = jnp.dot(hb16, wba_ref[...], preferred_element_type=F32)
    beta_ref[...] = _sigmoid(zba[:, :N_HEAD])
    gcol_ref[...] = -jnp.exp(alog_ref[...]) * _softplus(zba[:, N_HEAD:] + dtb_ref[...])
    if prompt:
        zbat = lax.dot_general(wbat_ref[...], hb16, _NT, preferred_element_type=F32)
        g_row = -jnp.exp(alogc_ref[...]) * _softplus(zbat[N_HEAD:, :] + dtbc_ref[...])
        for j in range(rows // DN_CHUNK):
            grow_ref[j] = g_row[:, j * DN_CHUNK:(j + 1) * DN_CHUNK]


def _in_call(x, sc, sh, w, *, prompt, n_seq, halo_b_in, halo_c_in):
    n_tok, d_model = x.shape
    if prompt:
        rows, stride = TM, 1
        n_tiles = n_tok // rows
        tiles_per_seq = n_tiles // n_seq
        halo_b = halo_c = SUBLANE
        mod_spec = pl.BlockSpec((1, 1, d_model), lambda t: (t // tiles_per_seq, 0, 0))
    else:
        rows, stride = n_tok, n_seq
        n_tiles, tiles_per_seq = 1, 1
        halo_b, halo_c = (CONV_B - 1) * n_seq, (CONV_C - 1) * n_seq
        mod_spec = pl.BlockSpec((1, rows, d_model), lambda t: (0, 0, 0))
    n_grp = n_tiles // tiles_per_seq

    def tok(width):
        return pl.BlockSpec((rows, width), lambda t: (t, 0))

    def per_seq(r, width):
        return pl.BlockSpec((1, r, width), lambda t: (t // tiles_per_seq, 0, 0))

    in_specs = [tok(d_model), mod_spec, mod_spec, _const_spec((1, d_model)),
                _const_spec(w['wa'].shape), _const_spec(w['wb'].shape), _const_spec(w['wc'].shape),
                _const_spec(w['wba'].shape), _const_spec(w['wbat'].shape),
                _const_spec((1, W_BR)), _const_spec((1, W_BR)),
                _const_spec((CONV_B, W_BR)), _const_spec((CONV_C, 3 * W_BR)),
                _const_spec((1, N_HEAD)), _const_spec((1, N_HEAD)),
                _const_spec((N_HEAD, 1)), _const_spec((N_HEAD, 1)),
                _const_spec((W_BR, W_BR)),
                per_seq(halo_b, W_BR), per_seq(halo_c, 3 * W_BR)]
    out_specs = [tok(W_BR), tok(W_BR), tok(W_BR), tok(W_BR), per_seq(halo_b, W_BR),
                 tok(W_BR), tok(W_BR), tok(W_BR), tok(W_BR), tok(N_HEAD), tok(N_HEAD),
                 per_seq(halo_c, 3 * W_BR)]
    out_shape = [jax.ShapeDtypeStruct((n_tok, W_BR), F32)] * 4 + [
        jax.ShapeDtypeStruct((n_grp, halo_b, W_BR), F32)] + [
        jax.ShapeDtypeStruct((n_tok, W_BR), F32)] * 4 + [
        jax.ShapeDtypeStruct((n_tok, N_HEAD), F32)] * 2 + [
        jax.ShapeDtypeStruct((n_grp, halo_c, 3 * W_BR), F32)]
    if prompt:
        cpt = rows // DN_CHUNK
        out_specs += [pl.BlockSpec((1, rows, W_BR), lambda t: (t, 0, 0)),
                      pl.BlockSpec((1, W_BR, rows), lambda t: (t, 0, 0)),
                      pl.BlockSpec((1, 1, W_BR), lambda t: (t, 0, 0)),
                      pl.BlockSpec((cpt, N_HEAD, DN_CHUNK), lambda t: (t, 0, 0))]
        out_shape += [jax.ShapeDtypeStruct((n_tiles, rows, W_BR), BF16),
                      jax.ShapeDtypeStruct((n_tiles, W_BR, rows), BF16),
                      jax.ShapeDtypeStruct((n_tiles, 1, W_BR), F32),
                      jax.ShapeDtypeStruct((n_tiles * cpt, N_HEAD, DN_CHUNK), F32)]
    kern = functools.partial(_in_kernel, prompt=prompt, rows=rows, tiles_per_seq=tiles_per_seq,
                             stride=stride, halo_b=halo_b, halo_c=halo_c)
    return pl.pallas_call(
        kern, grid=(n_tiles,), in_specs=in_specs, out_specs=out_specs, out_shape=out_shape,
        scratch_shapes=[pltpu.VMEM((halo_b + rows, W_BR), F32),
                        pltpu.VMEM((halo_c + rows, 3 * W_BR), F32)],
        compiler_params=_cparams(1),
    )(x, sc, sh, w['norm1_g'], w['wa'], w['wb'], w['wc'], w['wba'], w['wbat'], w['qn_g'], w['kn_g'],
      w['conv_b_w'], w['dn_conv_w'], w['a_log_row'], w['dt_bias_row'], w['a_log_col'],
      w['dt_bias_col'], w['gsum'], halo_b_in, halo_c_in)


def _gate_scores(a, b):
    return lax.dot_general(a.astype(BF16), b.astype(BF16), _NT, preferred_element_type=F32)


def _top3_mask_cols(gates, own, n_blk):
    blk = lax.broadcasted_iota(jnp.int32, gates.shape, 0)
    g = jnp.where(blk < own, gates, NEG)
    mask = jnp.full(gates.shape, NEG, F32)
    for _ in range(MOBA_TOPK):
        mx = jnp.max(g, axis=0, keepdims=True)
        idx = jnp.min(jnp.where(g == mx, blk, n_blk), axis=0, keepdims=True)
        hit = blk == idx
        mask = jnp.where(hit & (idx < own), 0.0, mask)
        g = jnp.where(hit, -jnp.inf, g)
    return mask


def _attn_prompt_kernel(q_ref, k_ref, vt_ref, kmean_ref, bias_ref, o_ref, sel_ref, s_ref,
                        *, n_blk, n_dist):
    qt = pl.program_id(2)
    qf = q_ref[0]
    lane = lax.broadcasted_iota(jnp.int32, qf.shape, 1)
    ones = jnp.ones((16, MOBA_BLOCK), BF16)
    qbs = []
    for hh in range(2):
        qh = jnp.where(lane // D_HEAD == hh, qf, 0.0)
        sel_ref[hh] = _top3_mask_cols(_gate_scores(kmean_ref[0], qh), qt, n_blk)
        qbs.append(qh.astype(BF16))

    def lhs_v(hh, n):
        return jnp.concatenate([vt_ref[0, n, hh * D_HEAD:(hh + 1) * D_HEAD, :], ones], axis=0)

    def scores(hh, n):
        return lax.dot_general(k_ref[0, n], qbs[hh], _NT, preferred_element_type=F32)

    def masked_scores(hh, n):
        d = jnp.minimum(qt - n, n_dist - 1)
        return scores(hh, n) + bias_ref[hh, d] + sel_ref[hh, pl.ds(n, 1), :]

    carry = []
    for hh in range(2):
        s = scores(hh, qt) + bias_ref[hh, 0]
        m = jnp.max(s, axis=0, keepdims=True)
        p = jnp.exp(s - m).astype(BF16)
        carry.append((m, jnp.dot(lhs_v(hh, qt), p, preferred_element_type=F32)))
        s_ref[0, hh] = masked_scores(hh, 0)
    carry = tuple(carry)

    def step(n, carry):
        slot = n % 2
        for hh in range(2):
            s_ref[1 - slot, hh] = masked_scores(hh, n + 1)
        out = []
        for hh in range(2):
            m, acc = carry[hh]
            s = s_ref[slot, hh]
            m_new = jnp.maximum(m, jnp.max(s, axis=0, keepdims=True))
            p = jnp.exp(s - m_new).astype(BF16)
            acc = jnp.exp(m - m_new) * acc + jnp.dot(lhs_v(hh, n), p, preferred_element_type=F32)
            out.append((m_new, acc))
        return tuple(out)

    carry = lax.fori_loop(0, qt, step, carry)
    outs = [acc[:D_HEAD] / acc[D_HEAD:D_HEAD + 1] for _, acc in carry]
    o_ref[0] = jnp.concatenate(outs, axis=0).T


def _attn_prompt_call(q, k_bf, vt_bf, kmean, bias_tab, n_seq):
    n_tok = q.shape[0]
    t_len = n_tok // n_seq
    n_blk = t_len // MOBA_BLOCK
    n_dist = bias_tab.shape[1]
    q3 = q.reshape(n_seq, t_len, W_BR)
    k4 = k_bf.reshape(n_seq, n_blk, MOBA_BLOCK, W_BR)
    vt4 = vt_bf.reshape(n_seq, n_blk, W_BR, MOBA_BLOCK)
    km3 = kmean.reshape(n_seq, n_blk, W_BR)
    kern = functools.partial(_attn_prompt_kernel, n_blk=n_blk, n_dist=n_dist)
    out = pl.pallas_call(
        kern,
        grid=(n_seq, N_HEAD // 2, n_blk),
        in_specs=[pl.BlockSpec((1, TM, 128), lambda b, hp, t: (b, t, hp)),
                  pl.BlockSpec((1, n_blk, MOBA_BLOCK, 128), lambda b, hp, t: (b, 0, 0, hp)),
                  pl.BlockSpec((1, n_blk, 128, MOBA_BLOCK), lambda b, hp, t: (b, 0, hp, 0)),
                  pl.BlockSpec((1, n_blk, 128), lambda b, hp, t: (b, 0, hp)),
                  pl.BlockSpec((2, n_dist, MOBA_BLOCK, MOBA_BLOCK), lambda b, hp, t: (hp, 0, 0, 0))],
        out_specs=pl.BlockSpec((1, TM, 128), lambda b, hp, t: (b, t, hp)),
        out_shape=jax.ShapeDtypeStruct((n_seq, t_len, W_BR), F32),
        scratch_shapes=[pltpu.VMEM((2, n_blk, TM), F32),
                        pltpu.VMEM((2, 2, MOBA_BLOCK, TM), F32)],
        compiler_params=_cparams(3),
    )(q3, k4, vt4, km3, bias_tab)
    return out.reshape(n_tok, W_BR)


def _stack_heads(x):
    t_len = x.shape[0]
    tiled = jnp.concatenate([x] * N_HEAD, axis=0)
    row = lax.broadcasted_iota(jnp.int32, tiled.shape, 0)
    lane = lax.broadcasted_iota(jnp.int32, tiled.shape, 1)
    return jnp.where(row // t_len == lane // D_HEAD, tiled, 0.0)


def _unstack_heads(y, t_len):
    row = lax.broadcasted_iota(jnp.int32, y.shape, 0)
    lane = lax.broadcasted_iota(jnp.int32, y.shape, 1)
    y = jnp.where(row // t_len == lane // D_HEAD, y, 0.0)
    out = y[0:t_len]
    for h in range(1, N_HEAD):
        out = out + y[h * t_len:(h + 1) * t_len]
    return out


def _attn_sample_kernel(pt_ref, q_ref, kn_ref, vn_ref, ck0_ref, ck1_ref, cv0_ref, cv1_ref,
                        bias_ref, bown_ref, o_ref,
                        qf_ref, qb_ref, kmean_ref, m_ref, l_ref, oacc_ref, *, n_blk, t_len):
    del pt_ref
    n = pl.program_id(1)
    n_row = N_HEAD * t_len

    @pl.when(n == 0)
    def _():
        qs = _stack_heads(q_ref[0])
        qf_ref[...] = qs
        qb_ref[...] = qs.astype(BF16)
        m_ref[...] = jnp.zeros(m_ref.shape, F32)
        l_ref[...] = jnp.zeros(l_ref.shape, F32)

    kblk = jnp.concatenate([ck0_ref[0, 0], ck1_ref[0, 0]], axis=0)
    vblk = jnp.concatenate([cv0_ref[0, 0], cv1_ref[0, 0]], axis=0)
    kmean_ref[pl.ds(n, 1), :] = jnp.sum(kblk, axis=0, keepdims=True) * (1.0 / MOBA_BLOCK)
    s = lax.dot_general(qb_ref[...], kblk.astype(BF16), _NT, preferred_element_type=F32) + bias_ref[n]
    m = jnp.max(s, axis=1, keepdims=True)
    p = jnp.exp(s - m)
    l = jnp.sum(p, axis=1, keepdims=True)
    oacc_ref[n] = jnp.dot(p.astype(BF16), vblk.astype(BF16), preferred_element_type=F32)
    col = lax.broadcasted_iota(jnp.int32, m_ref.shape, 1)
    m_ref[...] = jnp.where(col == n, m, m_ref[...])
    l_ref[...] = jnp.where(col == n, l, l_ref[...])

    @pl.when(n == n_blk - 1)
    def _():
        gates = _gate_scores(qf_ref[...], kmean_ref[...])
        blk = lax.broadcasted_iota(jnp.int32, gates.shape, 1)
        g = gates
        sel = jnp.zeros(gates.shape, jnp.bool_)
        for _ in range(MOBA_TOPK):
            mx = jnp.max(g, axis=1, keepdims=True)
            idx = jnp.min(jnp.where(g == mx, blk, n_blk), axis=1, keepdims=True)
            hit = blk == idx
            sel = sel | hit
            g = jnp.where(hit, -jnp.inf, g)
        m_all = m_ref[:, 0:n_blk]
        l_all = l_ref[:, 0:n_blk]

        pad = jnp.zeros((128 - t_len, W_BR), F32)
        kown = jnp.concatenate([kn_ref[0], pad], axis=0).astype(BF16)
        vown = jnp.concatenate([vn_ref[0], pad], axis=0).astype(BF16)
        s_own = lax.dot_general(qb_ref[...], kown, _NT, preferred_element_type=F32) + bown_ref[...]
        m_own = jnp.max(s_own, axis=1, keepdims=True)
        m_tot = jnp.maximum(m_own, jnp.max(jnp.where(sel, m_all, -jnp.inf), axis=1, keepdims=True))
        p_own = jnp.exp(s_own - m_tot)
        wgt = jnp.where(sel, jnp.exp(m_all - m_tot), 0.0)
        denom = jnp.sum(p_own, axis=1, keepdims=True) + jnp.sum(wgt * l_all, axis=1, keepdims=True)
        acc = jnp.dot(p_own.astype(BF16), vown, preferred_element_type=F32)
        for i in range(n_blk):
            acc = acc + wgt[:, i:i + 1] * oacc_ref[i]
        o_ref[0] = _unstack_heads(acc / denom, t_len)


def _attn_sample_call(page_table, q, k_new, v_new, cache_k, cache_v, layer, bias_tab, bias_own):
    n_seq, t_len, _ = q.shape
    n_pages = page_table.shape[1]
    page = cache_k.shape[2]
    n_blk = n_pages * page // MOBA_BLOCK
    n_row = N_HEAD * t_len
    tok = pl.BlockSpec((1, t_len, W_BR), lambda b, n, pt: (b, 0, 0))

    def page_spec(j):
        return pl.BlockSpec((1, 1, page, W_BR), lambda b, n, pt: (layer, pt[b, 2 * n + j], 0, 0))

    kern = functools.partial(_attn_sample_kernel, n_blk=n_blk, t_len=t_len)
    return pl.pallas_call(
        kern,
        grid_spec=pltpu.PrefetchScalarGridSpec(
            num_scalar_prefetch=1,
            grid=(n_seq, n_blk),
            in_specs=[tok, tok, tok, page_spec(0), page_spec(1), page_spec(0), page_spec(1),
                      pl.BlockSpec(bias_tab.shape, lambda b, n, pt: (0, 0, 0)),
                      pl.BlockSpec(bias_own.shape, lambda b, n, pt: (0, 0))],
            out_specs=tok,
            scratch_shapes=[pltpu.VMEM((n_row, W_BR), F32), pltpu.VMEM((n_row, W_BR), BF16),
                            pltpu.VMEM((n_blk, W_BR), F32), pltpu.VMEM((n_row, 128), F32),
                            pltpu.VMEM((n_row, 128), F32), pltpu.VMEM((n_blk, n_row, W_BR), F32)]),
        out_shape=jax.ShapeDtypeStruct((n_seq, t_len, W_BR), F32),
        compiler_params=_cparams(2),
    )(page_table, q, k_new, v_new, cache_k, cache_k, cache_v, cache_v, bias_tab, bias_own)


SOLVE_BASE = 4


def _unit_lower_inverse(a, size):
    n = a.shape[-1]
    ri = lax.broadcasted_iota(jnp.int32, (n, n), 0)
    ci = lax.broadcasted_iota(jnp.int32, (n, n), 1)
    eye = (ri == ci).astype(F32)

    def same_block(s):
        return (ri // s) == (ci // s)

    d = jnp.where(same_block(SOLVE_BASE), a, 0.0)
    inv = _mm3(eye - d, eye + _mm3(d, d))
    s = SOLVE_BASE
    while s < size:
        off = jnp.where(same_block(2 * s) & jnp.logical_not(same_block(s)), a, 0.0)
        inv = inv - _mm3(_mm3(inv, off), inv)
        s *= 2
    return inv


def _gdn_prompt_kernel(q_ref, k_ref, v_ref, beta_ref, gcol_ref, grow_ref, o_ref, s_ref, state_ref,
                       *, chunks):
    @pl.when(pl.program_id(1) == 0)
    def _():
        state_ref[...] = jnp.zeros(state_ref.shape, F32)

    ri = lax.broadcasted_iota(jnp.int32, (DN_CHUNK, DN_CHUNK), 0)
    ci = lax.broadcasted_iota(jnp.int32, (DN_CHUNK, DN_CHUNK), 1)
    tril = ri >= ci
    strict = ri > ci
    lower = tril.astype(F32)

    def chunk(c, carry):
        r0 = pl.multiple_of(c * DN_CHUNK, DN_CHUNK)
        qa = q_ref[0, pl.ds(r0, DN_CHUNK), :] * (D_HEAD ** -0.5)
        ka = k_ref[0, pl.ds(r0, DN_CHUNK), :]
        va = v_ref[0, pl.ds(r0, DN_CHUNK), :]
        beta = beta_ref[0, pl.ds(r0, DN_CHUNK), :]
        gcum_col = jnp.dot(lower, gcol_ref[0, pl.ds(r0, DN_CHUNK), :], precision=HIGHEST,
                           preferred_element_type=F32)
        gcum_row = lax.dot_general(grow_ref[c], lower, _NT, precision=HIGHEST,
                                   preferred_element_type=F32)
        def heads(x):
            return jnp.stack([x[:, h * D_HEAD:(h + 1) * D_HEAD] for h in range(N_HEAD)], axis=0)

        q, k, v = heads(qa), heads(ka), heads(va)
        kt = ka.T.reshape(N_HEAD, D_HEAD, DN_CHUNK)
        gc = jnp.stack([gcum_col[:, h:h + 1] for h in range(N_HEAD)], axis=0)
        gr = jnp.stack([gcum_row[h:h + 1, :] for h in range(N_HEAD)], axis=0)
        b = jnp.stack([beta[:, h:h + 1] for h in range(N_HEAD)], axis=0)
        decay = jnp.where(tril, jnp.exp(jnp.where(tril, gc - gr, 0.0)), 0.0)
        eg = jnp.exp(gc)
        glast = gr[:, :, DN_CHUNK - 1:DN_CHUNK]
        kb = k * b
        a = jnp.where(strict, _dot(kb, kt) * decay, 0.0)
        intra = _dot(q, kt) * decay
        x = _mm3(_unit_lower_inverse(a, DN_CHUNK), jnp.concatenate([v * b, kb * eg], axis=2))
        u, w = x[:, :, :D_HEAD], x[:, :, D_HEAD:]
        s = state_ref[...]
        v_new = u - _dot(w, s)
        o = _dot(q * eg, s) + _dot(intra, v_new)
        state_ref[...] = s * jnp.exp(glast) + _dot(kt * jnp.exp(glast - gr), v_new)
        o_ref[0, pl.ds(r0, DN_CHUNK), :] = jnp.concatenate([o[h] for h in range(N_HEAD)], axis=1)
        return carry

    lax.fori_loop(0, chunks, chunk, 0)
    s_ref[0] = state_ref[...]


def _gdn_prompt_call(qc, kc, vc, beta, gcol, grow, n_seq):
    n_tok = qc.shape[0]
    t_len = n_tok // n_seq
    rows = TM
    chunks = rows // DN_CHUNK
    n_tiles = t_len // rows

    def tok(width):
        return pl.BlockSpec((1, rows, width), lambda b, t: (b, t, 0))

    o, s = pl.pallas_call(
        functools.partial(_gdn_prompt_kernel, chunks=chunks),
        grid=(n_seq, n_tiles),
        in_specs=[tok(W_BR), tok(W_BR), tok(W_BR), tok(N_HEAD), tok(N_HEAD),
                  pl.BlockSpec((chunks, N_HEAD, DN_CHUNK), lambda b, t: (b * n_tiles + t, 0, 0))],
        out_specs=[tok(W_BR), pl.BlockSpec((1, N_HEAD, D_HEAD, D_HEAD), lambda b, t: (b, 0, 0, 0))],
        out_shape=[jax.ShapeDtypeStruct((n_seq, t_len, W_BR), F32),
                   jax.ShapeDtypeStruct((n_seq, N_HEAD, D_HEAD, D_HEAD), F32)],
        scratch_shapes=[pltpu.VMEM((N_HEAD, D_HEAD, D_HEAD), F32)],
        compiler_params=_cparams(2),
    )(qc.reshape(n_seq, t_len, W_BR), kc.reshape(n_seq, t_len, W_BR), vc.reshape(n_seq, t_len, W_BR),
      beta.reshape(n_seq, t_len, N_HEAD), gcol.reshape(n_seq, t_len, N_HEAD), grow)
    return o.reshape(n_tok, W_BR), s


def _gdn_sample_kernel(q_ref, k_ref, v_ref, bcol_ref, gcol_ref, grow_ref, s0_ref, o_ref, s_ref,
                       *, t_len):
    n_row = N_HEAD * t_len
    ri = lax.broadcasted_iota(jnp.int32, (n_row, n_row), 0)
    ci = lax.broadcasted_iota(jnp.int32, (n_row, n_row), 1)
    same = (ri // t_len) == (ci // t_len)
    tril = same & (ri >= ci)
    strict = same & (ri > ci)
    q = _stack_heads(q_ref[0]) * (D_HEAD ** -0.5)
    k = _stack_heads(k_ref[0])
    v = _stack_heads(v_ref[0])
    b = bcol_ref[0]
    g_lane = jnp.broadcast_to(gcol_ref[0], (n_row, 128))
    gc = jnp.dot(tril.astype(F32), g_lane, precision=HIGHEST, preferred_element_type=F32)[:, 0:1]
    gl = jnp.dot(same.astype(F32), g_lane, precision=HIGHEST, preferred_element_type=F32)[:, 0:1]
    g_sub = jnp.broadcast_to(grow_ref[0], (SUBLANE, n_row))
    gr = lax.dot_general(g_sub, tril.astype(F32), _NT, precision=HIGHEST,
                         preferred_element_type=F32)[0:1, :]
    decay = jnp.where(tril, jnp.exp(jnp.where(tril, gc - gr, 0.0)), 0.0)
    eg = jnp.exp(gc)
    kb = k * b
    a = jnp.where(strict, lax.dot_general(kb, k, _NT, preferred_element_type=F32) * decay, 0.0)
    intra = lax.dot_general(q, k, _NT, preferred_element_type=F32) * decay
    vb = v * b
    vb_c = vb[:, 0:D_HEAD]
    for h in range(1, N_HEAD):
        vb_c = vb_c + vb[:, h * D_HEAD:(h + 1) * D_HEAD]
    x = _mm3(_unit_lower_inverse(a, t_len), jnp.concatenate([kb * eg, vb_c], axis=1))
    w, u = x[:, :W_BR], x[:, W_BR:]
    s0 = s0_ref[0]
    v_new = u - jnp.dot(w, s0, preferred_element_type=F32)
    o = jnp.dot(q * eg, s0, preferred_element_type=F32) + jnp.dot(intra, v_new, preferred_element_type=F32)
    er = lax.broadcasted_iota(jnp.int32, (W_BR, n_row), 0)
    ec = lax.broadcasted_iota(jnp.int32, (W_BR, n_row), 1)
    pick = ((er // D_HEAD) == (ec // t_len)) & ((ec % t_len) == t_len - 1)
    gl_state = jnp.dot(pick.astype(F32), jnp.broadcast_to(gc, (n_row, 128)), precision=HIGHEST,
                       preferred_element_type=F32)[:, 0:1]
    s_ref[0] = s0 * jnp.exp(gl_state) + lax.dot_general(
        k * jnp.exp(gl - gc), v_new, _TN, preferred_element_type=F32)
    o_ref[0] = _unstack_heads(jnp.concatenate([o] * N_HEAD, axis=1), t_len)


def _gdn_sample_call(qc, kc, vc, bcol, gcol, grow, s0):
    n_seq, t_len, _ = qc.shape
    n_row = N_HEAD * t_len
    assert t_len >= SOLVE_BASE and t_len & (t_len - 1) == 0
    tok = pl.BlockSpec((1, t_len, W_BR), lambda b: (b, 0, 0))
    colv = pl.BlockSpec((1, n_row, 1), lambda b: (b, 0, 0))
    st = pl.BlockSpec((1, W_BR, D_HEAD), lambda b: (b, 0, 0))
    return pl.pallas_call(
        functools.partial(_gdn_sample_kernel, t_len=t_len),
        grid=(n_seq,),
        in_specs=[tok, tok, tok, colv, colv, pl.BlockSpec((1, 1, n_row), lambda b: (b, 0, 0)), st],
        out_specs=[tok, st],
        out_shape=[jax.ShapeDtypeStruct((n_seq, t_len, W_BR), F32),
                   jax.ShapeDtypeStruct((n_seq, W_BR, D_HEAD), F32)],
        compiler_params=_cparams(1),
    )(qc, kc, vc, bcol, gcol, grow, s0)


def _out_kernel(x_ref, sc_ref, sh_ref, gate_ref, g1_ref, attn_ref, yb_ref, od_ref, og_ref, dng_ref,
                wmg_ref, woa_ref, wob_ref, woc_ref, wo_ref, gsum_ref, o_ref):
    x = x_ref[...]
    h = _mod_norm(x, g1_ref[...], sc_ref[0], sh_ref[0]).astype(BF16)
    mg = _sigmoid(jnp.dot(h, wmg_ref[...], preferred_element_type=F32))
    d = x.shape[1]
    ya = jnp.dot(attn_ref[...].astype(BF16), woa_ref[...], preferred_element_type=F32)
    yb = jnp.dot(yb_ref[...].astype(BF16), wob_ref[...], preferred_element_type=F32)
    od = od_ref[...]
    oc = (od * lax.rsqrt(_group_sumsq(od, gsum_ref[...]) * (1.0 / D_HEAD) + EPS) * dng_ref[...]) \
        * _silu(og_ref[...])
    yc = jnp.dot(oc.astype(BF16), woc_ref[...], preferred_element_type=F32)
    merged = mg[:, :d] * ya + mg[:, d:2 * d] * yb + mg[:, 2 * d:] * yc
    o_ref[...] = x + gate_ref[0] * jnp.dot(merged.astype(BF16), wo_ref[...], preferred_element_type=F32)


def _token_grid(n_tok, n_seq, prompt, d_model):
    if prompt:
        rows = TM
        n_tiles = n_tok // rows
        tiles_per_seq = n_tiles // n_seq
        mod_spec = pl.BlockSpec((1, 1, d_model), lambda t: (t // tiles_per_seq, 0, 0))
    else:
        rows, n_tiles, tiles_per_seq = n_tok, 1, 1
        mod_spec = pl.BlockSpec((1, rows, d_model), lambda t: (0, 0, 0))
    return rows, n_tiles, tiles_per_seq, mod_spec


def _out_call(x, sc, sh, gate, attn, yb, od, og, w, *, prompt, n_seq):
    n_tok, d_model = x.shape
    rows, n_tiles, _, mod_spec = _token_grid(n_tok, n_seq, prompt, d_model)

    def tok(width):
        return pl.BlockSpec((rows, width), lambda t: (t, 0))

    return pl.pallas_call(
        _out_kernel, grid=(n_tiles,),
        in_specs=[tok(d_model), mod_spec, mod_spec, mod_spec, _const_spec((1, d_model)),
                  tok(W_BR), tok(W_BR), tok(W_BR), tok(W_BR), _const_spec((1, W_BR)),
                  _const_spec(w['wmg'].shape), _const_spec(w['woa'].shape), _const_spec(w['wob'].shape),
                  _const_spec(w['woc'].shape), _const_spec(w['wo'].shape), _const_spec((W_BR, W_BR))],
        out_specs=tok(d_model),
        out_shape=jax.ShapeDtypeStruct((n_tok, d_model), F32),
        compiler_params=_cparams(1),
    )(x, sc, sh, gate, w['norm1_g'], attn, yb, od, og, w['dn_norm_g'], w['wmg'], w['woa'], w['wob'],
      w['woc'], w['wo'], w['gsum'])


def _ffn_kernel(x_ref, sc_ref, sh_ref, gate_ref, g2_ref, wg_ref, wu_ref, wd_ref, cw_ref, halo_ref,
                o_ref, tail_ref, xp_ref, *, rows, tiles_per_seq, stride, halo):
    first = (pl.program_id(0) % tiles_per_seq) == 0
    x = x_ref[...]
    h = _mod_norm(x, g2_ref[...], sc_ref[0], sh_ref[0]).astype(BF16)
    pre = jnp.dot(h, wg_ref[...], preferred_element_type=F32)
    hg, tail = _causal_conv(xp_ref, pre, cw_ref, halo_ref, first,
                            width=CONV_F, halo=halo, stride=stride, rows=rows)
    tail_ref[0] = tail
    act = _silu(hg) * jnp.dot(h, wu_ref[...], preferred_element_type=F32)
    o_ref[...] = x + gate_ref[0] * jnp.dot(act.astype(BF16), wd_ref[...], preferred_element_type=F32)


def _ffn_call(x, sc, sh, gate, w, halo_in, *, prompt, n_seq):
    n_tok, d_model = x.shape
    d_ff = w['wgate'].shape[1]
    rows, n_tiles, tiles_per_seq, mod_spec = _token_grid(n_tok, n_seq, prompt, d_model)
    stride = 1 if prompt else n_seq
    halo = SUBLANE if prompt else (CONV_F - 1) * n_seq
    n_grp = n_tiles // tiles_per_seq
    tok = pl.BlockSpec((rows, d_model), lambda t: (t, 0))
    per_seq = pl.BlockSpec((1, halo, d_ff), lambda t: (t // tiles_per_seq, 0, 0))
    kern = functools.partial(_ffn_kernel, rows=rows, tiles_per_seq=tiles_per_seq, stride=stride, halo=halo)
    return pl.pallas_call(
        kern, grid=(n_tiles,),
        in_specs=[tok, mod_spec, mod_spec, mod_spec, _const_spec((1, d_model)),
                  _const_spec(w['wgate'].shape), _const_spec(w['wup'].shape), _const_spec(w['wdown'].shape),
                  _const_spec((CONV_F, d_ff)), per_seq],
        out_specs=[tok, per_seq],
        out_shape=[jax.ShapeDtypeStruct((n_tok, d_model), F32),
                   jax.ShapeDtypeStruct((n_grp, halo, d_ff), F32)],
        scratch_shapes=[pltpu.VMEM((halo + rows, d_ff), F32)],
        compiler_params=_cparams(1),
    )(x, sc, sh, gate, w['norm2_g'], w['wgate'], w['wup'], w['wdown'], w['ffn_conv_w'], halo_in)


def _t5_bucket(rel):
    n = jnp.maximum(rel, 0)
    max_exact = REL_BUCKETS // 2
    nf = jnp.maximum(n, 1).astype(F32)
    large = max_exact + (jnp.log(nf / max_exact) / math.log(REL_MAX_DIST / max_exact)
                         * (REL_BUCKETS - max_exact)).astype(jnp.int32)
    large = jnp.minimum(large, REL_BUCKETS - 1)
    return jnp.where(n < max_exact, n, large)


def _bias_tables(rel_bias, n_blk_prompt, past_len, t_s):
    blk = MOBA_BLOCK
    n_dist = min(n_blk_prompt, -(-(REL_MAX_DIST + blk - 1) // blk) + 1)
    d = jnp.arange(n_dist, dtype=jnp.int32)[:, None, None]
    ki = jnp.arange(blk, dtype=jnp.int32)[None, :, None]
    qj = jnp.arange(blk, dtype=jnp.int32)[None, None, :]
    rel = d * blk + qj - ki
    tab_p = jnp.where(rel[..., None] >= 0, rel_bias[_t5_bucket(rel)], NEG)
    tab_p = jnp.transpose(tab_p, (3, 0, 1, 2))
    n_blk_s = past_len // blk
    nb = jnp.arange(n_blk_s, dtype=jnp.int32)[:, None, None]
    tq = jnp.arange(t_s, dtype=jnp.int32)[None, :, None]
    ko = jnp.arange(blk, dtype=jnp.int32)[None, None, :]
    rel_s = past_len + tq - nb * blk - ko
    tab_s = jnp.transpose(rel_bias[_t5_bucket(rel_s)], (0, 3, 1, 2)).reshape(n_blk_s, N_HEAD * t_s, blk)
    ko2 = jnp.arange(128, dtype=jnp.int32)[None, :]
    rel_o = jnp.arange(t_s, dtype=jnp.int32)[:, None] - ko2
    tab_o = jnp.where(rel_o[..., None] >= 0, rel_bias[_t5_bucket(rel_o)], NEG)
    tab_o = jnp.transpose(tab_o, (2, 0, 1)).reshape(N_HEAD * t_s, 128)
    return tab_p.astype(F32), tab_s.astype(F32), tab_o.astype(F32)


def _layer_weights(l, p, gsum):
    w_in = p['w_in'][l]
    cuts = [0, 3 * W_BR, 6 * W_BR, 10 * W_BR, 10 * W_BR + 2 * N_HEAD]
    wba = w_in[:, cuts[3]:cuts[4]].astype(BF16)
    row = lambda a: a.reshape(1, -1)
    return {
        'wa': w_in[:, cuts[0]:cuts[1]].astype(BF16),
        'wb': w_in[:, cuts[1]:cuts[2]].astype(BF16),
        'wc': w_in[:, cuts[2]:cuts[3]].astype(BF16),
        'wba': wba, 'wbat': wba.T,
        'wmg': w_in[:, cuts[4]:].astype(BF16),
        'norm1_g': row(p['norm1_g'][l]), 'norm2_g': row(p['norm2_g'][l]),
        'qn_g': row(jnp.tile(p['qn_g'][l], N_HEAD)), 'kn_g': row(jnp.tile(p['kn_g'][l], N_HEAD)),
        'dn_norm_g': row(jnp.tile(p['dn_norm_g'][l], N_HEAD)),
        'conv_b_w': p['conv_b_w'][l], 'dn_conv_w': p['dn_conv_w'][l], 'ffn_conv_w': p['ffn_conv_w'][l],
        'a_log_row': row(p['dn_a_log'][l]), 'dt_bias_row': row(p['dn_dt_bias'][l]),
        'a_log_col': p['dn_a_log'][l].reshape(-1, 1), 'dt_bias_col': p['dn_dt_bias'][l].reshape(-1, 1),
        'woa': p['w_oa'][l].astype(BF16), 'wob': p['w_ob'][l].astype(BF16), 'woc': p['w_oc'][l].astype(BF16),
        'wo': p['w_o'][l].astype(BF16),
        'wgate': p['w_gate'][l].astype(BF16), 'wup': p['w_up'][l].astype(BF16),
        'wdown': p['w_down'][l].astype(BF16),
        'gsum': gsum,
    }


def kernel(x_prompt, x_sample, cache_k, cache_v, state_conv_b, state_conv_dn, state_dn, state_conv_ffn,
           page_table, c_prompt, c_sample, rel_bias, w_ada, b_ada, norm1_g, norm2_g, w_in, qn_g, kn_g,
           conv_b_w, dn_conv_w, dn_a_log, dn_dt_bias, dn_norm_g, w_oa, w_ob, w_oc, w_o, w_gate, w_up,
           ffn_conv_w, w_down):
    bp, tp, d_model = x_prompt.shape
    bs, ts, _ = x_sample.shape
    depth = w_in.shape[0]
    d_ff = w_gate.shape[2]
    n_pool, page = cache_k.shape[1], cache_k.shape[2]
    past_len = page_table.shape[1] * page
    assert tp % MOBA_BLOCK == 0 and past_len % MOBA_BLOCK == 0 and MOBA_BLOCK == 2 * page
    assert ts <= 128 and past_len // MOBA_BLOCK >= MOBA_TOPK and (bs * ts) % SUBLANE == 0
    p = dict(w_in=w_in, norm1_g=norm1_g, norm2_g=norm2_g, qn_g=qn_g, kn_g=kn_g, conv_b_w=conv_b_w,
             dn_conv_w=dn_conv_w, dn_a_log=dn_a_log, dn_dt_bias=dn_dt_bias, dn_norm_g=dn_norm_g,
             w_oa=w_oa, w_ob=w_ob, w_oc=w_oc, w_o=w_o, w_gate=w_gate, w_up=w_up,
             ffn_conv_w=ffn_conv_w, w_down=w_down)

    lane = jnp.arange(W_BR) // D_HEAD
    gsum = (lane[:, None] == lane[None, :]).astype(BF16)
    tab_p, tab_s, tab_o = _bias_tables(rel_bias, tp // MOBA_BLOCK, past_len, ts)

    n_c = bp + bs
    c_rows = -(-n_c // SUBLANE) * SUBLANE
    c_all = jnp.concatenate([c_prompt, c_sample, jnp.zeros((c_rows - n_c, d_model), F32)], axis=0)
    mods = _ada_call(c_all, w_ada, b_ada)

    ck = cache_k.reshape(depth, n_pool, page, W_BR)
    cv = cache_v.reshape(depth, n_pool, page, W_BR)

    xp = x_prompt.reshape(bp * tp, d_model)
    xs = jnp.transpose(x_sample, (1, 0, 2)).reshape(ts * bs, d_model)

    def to_tm(a):
        return jnp.transpose(a, (1, 0, 2)).reshape(1, a.shape[1] * bs, a.shape[2])

    def from_tm(a, r):
        return jnp.transpose(a.reshape(r, bs, a.shape[-1]), (1, 0, 2))

    zero_b = jnp.zeros((bp, SUBLANE, W_BR), F32)
    zero_c = jnp.zeros((bp, SUBLANE, 3 * W_BR), F32)
    zero_f = jnp.zeros((bp, SUBLANE, d_ff), F32)

    outs = [[] for _ in range(12)]
    for l in range(depth):
        w = _layer_weights(l, p, gsum)
        mod_p = [m.reshape(bp, 1, d_model) for m in jnp.split(mods[l, :bp], 6, axis=-1)]
        mod_s = [jnp.tile(m, (ts, 1)).reshape(1, ts * bs, d_model)
                 for m in jnp.split(mods[l, bp:bp + bs], 6, axis=-1)]

        sh1, sc1, g1, sh2, sc2, g2 = mod_p
        (q, k, v, yb, tail_b, qc, kc, vc, og, beta, gcol, tail_c, k_bf, vt_bf, kmean, grow) = _in_call(
            xp, sc1, sh1, w, prompt=True, n_seq=bp, halo_b_in=zero_b, halo_c_in=zero_c)
        attn = _attn_prompt_call(q, k_bf, vt_bf, kmean, tab_p, bp)
        od, s_p = _gdn_prompt_call(qc, kc, vc, beta, gcol, grow, bp)
        xp = _out_call(xp, sc1, sh1, g1, attn, yb, od, og, w, prompt=True, n_seq=bp)
        xp, tail_f = _ffn_call(xp, sc2, sh2, g2, w, zero_f, prompt=True, n_seq=bp)
        outs[0].append(k.reshape(bp, tp, N_HEAD, D_HEAD))
        outs[1].append(v.reshape(bp, tp, N_HEAD, D_HEAD))
        outs[4].append(tail_b[:, SUBLANE - (CONV_B - 1):])
        outs[6].append(tail_c[:, SUBLANE - (CONV_C - 1):])
        outs[8].append(s_p)
        outs[10].append(tail_f[:, SUBLANE - (CONV_F - 1):])

        sh1, sc1, g1, sh2, sc2, g2 = mod_s
        (q, k, v, yb, tail_b, qc, kc, vc, og, beta, gcol, tail_c) = _in_call(
            xs, sc1, sh1, w, prompt=False, n_seq=bs,
            halo_b_in=to_tm(state_conv_b[l]), halo_c_in=to_tm(state_conv_dn[l]))
        q_b, k_b, v_b = from_tm(q, ts), from_tm(k, ts), from_tm(v, ts)
        attn = _attn_sample_call(page_table, q_b, k_b, v_b, ck, cv, l, tab_s, tab_o)
        beta_ht = jnp.transpose(beta.reshape(ts, bs, N_HEAD), (1, 2, 0)).reshape(bs, N_HEAD * ts)
        g_ht = jnp.transpose(gcol.reshape(ts, bs, N_HEAD), (1, 2, 0)).reshape(bs, N_HEAD * ts)
        od, s_s = _gdn_sample_call(from_tm(qc, ts), from_tm(kc, ts), from_tm(vc, ts),
                                   beta_ht[:, :, None], g_ht[:, :, None], g_ht[:, None, :],
                                   state_dn[l].reshape(bs, W_BR, D_HEAD))
        attn_tm = jnp.transpose(attn, (1, 0, 2)).reshape(ts * bs, W_BR)
        od_tm = jnp.transpose(od, (1, 0, 2)).reshape(ts * bs, W_BR)
        xs = _out_call(xs, sc1, sh1, g1, attn_tm, yb, od_tm, og, w, prompt=False, n_seq=bs)
        xs, tail_f = _ffn_call(xs, sc2, sh2, g2, w, to_tm(state_conv_ffn[l]), prompt=False, n_seq=bs)
        outs[2].append(k_b.reshape(bs, ts, N_HEAD, D_HEAD))
        outs[3].append(v_b.reshape(bs, ts, N_HEAD, D_HEAD))
        outs[5].append(from_tm(tail_b[0], CONV_B - 1))
        outs[7].append(from_tm(tail_c[0], CONV_C - 1))
        outs[9].append(s_s.reshape(bs, N_HEAD, D_HEAD, D_HEAD))
        outs[11].append(from_tm(tail_f[0], CONV_F - 1))

    y_p = xp.reshape(bp, tp, d_model)
    y_s = from_tm(xs, ts)
    return (y_p, y_s) + tuple(jnp.stack(o) for o in outs)
```

```python
import functools
import math

import jax
import jax.numpy as jnp
from jax import lax
from jax.experimental import pallas as pl
from jax.experimental.pallas import tpu as pltpu

F32 = jnp.float32
BF16 = jnp.bfloat16

N_HEAD = 8
D_HEAD = 64
W_BR = N_HEAD * D_HEAD
MOBA_BLOCK = 256
MOBA_TOPK = 3
CONV_B = 3
CONV_C = 4
CONV_F = 3
DN_CHUNK = 64
REL_BUCKETS = 32
REL_MAX_DIST = 2048
N_BRANCH = 3
EPS = 1e-6
NEG = -1e30

TM = MOBA_BLOCK
SUBLANE = 8
VMEM_LIMIT = 56 * 1024 * 1024
HIGHEST = lax.Precision.HIGHEST

_NT = (((1,), (1,)), ((), ()))
_TN = (((0,), (0,)), ((), ()))


def _const_spec(shape):
    nd = len(shape)
    return pl.BlockSpec(shape, lambda *_: (0,) * nd, pipeline_mode=pl.Buffered(1))


def _cparams(n_grid):
    return pltpu.CompilerParams(dimension_semantics=("arbitrary",) * n_grid,
                                vmem_limit_bytes=VMEM_LIMIT)


def _sigmoid(x):
    return 1.0 / (1.0 + jnp.exp(-x))


def _silu(x):
    return x * _sigmoid(x)


def _softplus(x):
    return jnp.maximum(x, 0.0) + jnp.log1p(jnp.exp(-jnp.abs(x)))


def _split_dot(a, b_bf16):
    hi = a.astype(BF16)
    lo = (a - hi.astype(F32)).astype(BF16)
    return (jnp.dot(hi, b_bf16, preferred_element_type=F32)
            + jnp.dot(lo, b_bf16, preferred_element_type=F32))


def _dot(a, b):
    if a.ndim == 3:
        return lax.dot_general(a, b, (((2,), (1,)), ((0,), (0,))), preferred_element_type=F32)
    return jnp.dot(a, b, preferred_element_type=F32)


def _mm3(a, b):
    ah = a.astype(BF16)
    al = (a - ah.astype(F32)).astype(BF16)
    bh = b.astype(BF16)
    bl = (b - bh.astype(F32)).astype(BF16)
    return _dot(ah, bh) + _dot(ah, bl) + _dot(al, bh)


def _group_sumsq(y, gsum_bf16):
    return _split_dot(y * y, gsum_bf16)


def _mod_norm(x, gain, sc, sh):
    ms = jnp.mean(x * x, axis=-1, keepdims=True)
    return (x * lax.rsqrt(ms + EPS) * gain) * (1.0 + sc) + sh


def _causal_conv(xp_ref, cur, w_ref, halo_ref, first, *, width, halo, stride, rows):
    @pl.when(first)
    def _():
        xp_ref[0:halo, :] = halo_ref[0]

    xp_ref[halo:halo + rows, :] = cur
    off = halo - (width - 1) * stride
    y = xp_ref[off:off + rows, :] * w_ref[0:1, :]
    for i in range(1, width):
        off = halo - (width - 1 - i) * stride
        y = y + xp_ref[off:off + rows, :] * w_ref[i:i + 1, :]
    tail = xp_ref[rows:rows + halo, :]
    xp_ref[0:halo, :] = tail
    return y, tail


def _ada_kernel(c_ref, w_ref, b_ref, o_ref):
    c = _silu(c_ref[...]).astype(BF16)
    o_ref[0] = jnp.dot(c, w_ref[0].astype(BF16), preferred_element_type=F32) + b_ref[0]


def _ada_call(c_all, w_ada, b_ada):
    depth, d_model, n_mod = w_ada.shape
    rows = c_all.shape[0]
    bn = 1536
    return pl.pallas_call(
        _ada_kernel,
        grid=(depth, n_mod // bn),
        in_specs=[pl.BlockSpec((rows, d_model), lambda l, j: (0, 0)),
                  pl.BlockSpec((1, d_model, bn), lambda l, j: (l, 0, j)),
                  pl.BlockSpec((1, 1, bn), lambda l, j: (l, 0, j))],
        out_specs=pl.BlockSpec((1, rows, bn), lambda l, j: (l, 0, j)),
        out_shape=jax.ShapeDtypeStruct((depth, rows, n_mod), F32),
        compiler_params=_cparams(2),
    )(c_all, w_ada, b_ada.reshape(depth, 1, n_mod))


def _in_kernel(*refs, prompt, rows, tiles_per_seq, stride, halo_b, halo_c):
    (x_ref, sc_ref, sh_ref, g1_ref, wa_ref, wb_ref, wc_ref, wba_ref, wbat_ref, qn_ref, kn_ref,
     cbw_ref, cdw_ref, alog_ref, dtb_ref, alogc_ref, dtbc_ref, gsum_ref, hb_ref, hc_ref) = refs[:20]
    outs = refs[20:-2]
    xpb_ref, xpc_ref = refs[-2:]
    if prompt:
        (q_ref, k_ref, v_ref, yb_ref, tailb_ref, qc_ref, kc_ref, vc_ref, og_ref, beta_ref,
         gcol_ref, tailc_ref, kbf_ref, vt_ref, kmean_ref, grow_ref) = outs
    else:
        (q_ref, k_ref, v_ref, yb_ref, tailb_ref, qc_ref, kc_ref, vc_ref, og_ref, beta_ref,
         gcol_ref, tailc_ref) = outs

    first = (pl.program_id(0) % tiles_per_seq) == 0
    gsum = gsum_ref[...]
    h = _mod_norm(x_ref[...], g1_ref[...], sc_ref[0], sh_ref[0])
    hb16 = h.astype(BF16)

    za = jnp.dot(hb16, wa_ref[...], preferred_element_type=F32)
    qa, ka, va = za[:, :W_BR], za[:, W_BR:2 * W_BR], za[:, 2 * W_BR:]
    q = (qa * lax.rsqrt(_group_sumsq(qa, gsum) * (1.0 / D_HEAD) + EPS) * qn_ref[...]) * (D_HEAD ** -0.5)
    k = ka * lax.rsqrt(_group_sumsq(ka, gsum) * (1.0 / D_HEAD) + EPS) * kn_ref[...]
    q_ref[...] = q
    k_ref[...] = k
    v_ref[...] = va
    if prompt:
        kbf_ref[0] = k.astype(BF16)
        vt_ref[0] = va.T.astype(BF16)
        kmean_ref[0] = jnp.sum(k, axis=0, keepdims=True) * (1.0 / MOBA_BLOCK)

    zb = jnp.dot(hb16, wb_ref[...], preferred_element_type=F32)
    hb, bg, cg = zb[:, :W_BR], zb[:, W_BR:2 * W_BR], zb[:, 2 * W_BR:]
    uc, tail_b = _causal_conv(xpb_ref, cg * hb, cbw_ref, hb_ref, first,
                              width=CONV_B, halo=halo_b, stride=stride, rows=rows)
    yb_ref[...] = bg * uc
    tailb_ref[0] = tail_b

    zc = jnp.dot(hb16, wc_ref[...], preferred_element_type=F32)
    qkv, tail_c = _causal_conv(xpc_ref, zc[:, :3 * W_BR], cdw_ref, hc_ref, first,
                               width=CONV_C, halo=halo_c, stride=stride, rows=rows)
    tailc_ref[0] = tail_c
    og_ref[...] = zc[:, 3 * W_BR:]
    qkv = _silu(qkv)
    qc, kc, vc = qkv[:, :W_BR], qkv[:, W_BR:2 * W_BR], qkv[:, 2 * W_BR:]
    qc_ref[...] = qc * lax.rsqrt(_group_sumsq(qc, gsum) + EPS)
    kc_ref[...] = kc * lax.rsqrt(_group_sumsq(kc, gsum) + EPS)
    vc_ref[...] = vc
    zba = jnp.dot(hb16, wba_ref[...], preferred_element_type=F32)
    beta_ref[...] = _sigmoid(zba[:, :N_HEAD])
    gcol_ref[...] = -jnp.exp(alog_ref[...]) * _softplus(zba[:, N_HEAD:] + dtb_ref[...])
    if prompt:
        zbat = lax.dot_general(wbat_ref[...], hb16, _NT, preferred_element_type=F32)
        g_row = -jnp.exp(alogc_ref[...]) * _softplus(zbat[N_HEAD:, :] + dtbc_ref[...])
        for j in range(rows // DN_CHUNK):
            grow_ref[j] = g_row[:, j * DN_CHUNK:(j + 1) * DN_CHUNK]


def _in_call(x, sc, sh, w, *, prompt, n_seq, halo_b_in, halo_c_in):
    n_tok, d_model = x.shape
    if prompt:
        rows, stride = TM, 1
        n_tiles = n_tok // rows
        tiles_per_seq = n_tiles // n_seq
        halo_b = halo_c = SUBLANE
        mod_spec = pl.BlockSpec((1, 1, d_model), lambda t: (t // tiles_per_seq, 0, 0))
    else:
        rows, stride = n_tok, n_seq
        n_tiles, tiles_per_seq = 1, 1
        halo_b, halo_c = (CONV_B - 1) * n_seq, (CONV_C - 1) * n_seq
        mod_spec = pl.BlockSpec((1, rows, d_model), lambda t: (0, 0, 0))
    n_grp = n_tiles // tiles_per_seq

    def tok(width):
        return pl.BlockSpec((rows, width), lambda t: (t, 0))

    def per_seq(r, width):
        return pl.BlockSpec((1, r, width), lambda t: (t // tiles_per_seq, 0, 0))

    in_specs = [tok(d_model), mod_spec, mod_spec, _const_spec((1, d_model)),
                _const_spec(w['wa'].shape), _const_spec(w['wb'].shape), _const_spec(w['wc'].shape),
                _const_spec(w['wba'].shape), _const_spec(w['wbat'].shape),
                _const_spec((1, W_BR)), _const_spec((1, W_BR)),
                _const_spec((CONV_B, W_BR)), _const_spec((CONV_C, 3 * W_BR)),
                _const_spec((1, N_HEAD)), _const_spec((1, N_HEAD)),
                _const_spec((N_HEAD, 1)), _const_spec((N_HEAD, 1)),
                _const_spec((W_BR, W_BR)),
                per_seq(halo_b, W_BR), per_seq(halo_c, 3 * W_BR)]
    out_specs = [tok(W_BR), tok(W_BR), tok(W_BR), tok(W_BR), per_seq(halo_b, W_BR),
                 tok(W_BR), tok(W_BR), tok(W_BR), tok(W_BR), tok(N_HEAD), tok(N_HEAD),
                 per_seq(halo_c, 3 * W_BR)]
    out_shape = [jax.ShapeDtypeStruct((n_tok, W_BR), F32)] * 4 + [
        jax.ShapeDtypeStruct((n_grp, halo_b, W_BR), F32)] + [
        jax.ShapeDtypeStruct((n_tok, W_BR), F32)] * 4 + [
        jax.ShapeDtypeStruct((n_tok, N_HEAD), F32)] * 2 + [
        jax.ShapeDtypeStruct((n_grp, halo_c, 3 * W_BR), F32)]
    if prompt:
        cpt = rows // DN_CHUNK
        out_specs += [pl.BlockSpec((1, rows, W_BR), lambda t: (t, 0, 0)),
                      pl.BlockSpec((1, W_BR, rows), lambda t: (t, 0, 0)),
                      pl.BlockSpec((1, 1, W_BR), lambda t: (t, 0, 0)),
                      pl.BlockSpec((cpt, N_HEAD, DN_CHUNK), lambda t: (t, 0, 0))]
        out_shape += [jax.ShapeDtypeStruct((n_tiles, rows, W_BR), BF16),
                      jax.ShapeDtypeStruct((n_tiles, W_BR, rows), BF16),
                      jax.ShapeDtypeStruct((n_tiles, 1, W_BR), F32),
                      jax.ShapeDtypeStruct((n_tiles * cpt, N_HEAD, DN_CHUNK), F32)]
    kern = functools.partial(_in_kernel, prompt=prompt, rows=rows, tiles_per_seq=tiles_per_seq,
                             stride=stride, halo_b=halo_b, halo_c=halo_c)
    return pl.pallas_call(
        kern, grid=(n_tiles,), in_specs=in_specs, out_specs=out_specs, out_shape=out_shape,
        scratch_shapes=[pltpu.VMEM((halo_b + rows, W_BR), F32),
                        pltpu.VMEM((halo_c + rows, 3 * W_BR), F32)],
        compiler_params=_cparams(1),
    )(x, sc, sh, w['norm1_g'], w['wa'], w['wb'], w['wc'], w['wba'], w['wbat'], w['qn_g'], w['kn_g'],
      w['conv_b_w'], w['dn_conv_w'], w['a_log_row'], w['dt_bias_row'], w['a_log_col'],
      w['dt_bias_col'], w['gsum'], halo_b_in, halo_c_in)


def _gate_scores(a, b):
    return lax.dot_general(a.astype(BF16), b.astype(BF16), _NT, preferred_element_type=F32)


def _top3_mask_cols(gates, own, n_blk):
    blk = lax.broadcasted_iota(jnp.int32, gates.shape, 0)
    g = jnp.where(blk < own, gates, NEG)
    mask = jnp.full(gates.shape, NEG, F32)
    for _ in range(MOBA_TOPK):
        mx = jnp.max(g, axis=0, keepdims=True)
        idx = jnp.min(jnp.where(g == mx, blk, n_blk), axis=0, keepdims=True)
        hit = blk == idx
        mask = jnp.where(hit & (idx < own), 0.0, mask)
        g = jnp.where(hit, -jnp.inf, g)
    return mask


def _attn_prompt_kernel(q_ref, k_ref, vt_ref, kmean_ref, bias_ref, o_ref, sel_ref, s_ref, p_ref,
                        *, n_blk, n_dist):
    qt = pl.program_id(2)
    qf = q_ref[0]
    lane = lax.broadcasted_iota(jnp.int32, qf.shape, 1)
    ones = jnp.ones((16, MOBA_BLOCK), BF16)
    qbs = []
    for hh in range(2):
        qh = jnp.where(lane // D_HEAD == hh, qf, 0.0)
        sel_ref[hh] = _top3_mask_cols(_gate_scores(kmean_ref[0], qh), qt, n_blk)
        qbs.append(qh.astype(BF16))

    def lhs_v(hh, n):
        return jnp.concatenate([vt_ref[0, n, hh * D_HEAD:(hh + 1) * D_HEAD, :], ones], axis=0)

    def scores(hh, n):
        return lax.dot_general(k_ref[0, n], qbs[hh], _NT, preferred_element_type=F32)

    def masked_scores(hh, n):
        d = jnp.minimum(qt - n, n_dist - 1)
        return scores(hh, n) + bias_ref[hh, d] + sel_ref[hh, pl.ds(n, 1), :]

    carry = []
    for hh in range(2):
        s = scores(hh, qt) + bias_ref[hh, 0]
        m = jnp.max(s, axis=0, keepdims=True)
        p_ref[1, hh] = jnp.exp(s - m).astype(BF16)
        carry.append((m, jnp.ones_like(m), jnp.zeros((D_HEAD + 16, TM), F32)))
        s_ref[0, hh] = masked_scores(hh, 0)
    carry = tuple(carry)

    def pending_pv(hh, n, slot):
        v_blk = jnp.where(n == 0, qt, n - 1)
        return jnp.dot(lhs_v(hh, v_blk), p_ref[1 - slot, hh], preferred_element_type=F32)

    def step(n, carry):
        slot = n % 2
        pv = [pending_pv(hh, n, slot) for hh in range(2)]
        nxt = [masked_scores(hh, n + 1) for hh in range(2)]
        out = []
        for hh in range(2):
            m, alpha, acc = carry[hh]
            s = s_ref[slot, hh]
            m_new = jnp.maximum(m, jnp.max(s, axis=0, keepdims=True))
            p_ref[slot, hh] = jnp.exp(s - m_new).astype(BF16)
            out.append((m_new, jnp.exp(m - m_new), alpha * acc + pv[hh]))
        for hh in range(2):
            s_ref[1 - slot, hh] = nxt[hh]
        return tuple(out)

    carry = lax.fori_loop(0, qt, step, carry)
    outs = []
    for hh in range(2):
        _, alpha, acc = carry[hh]
        acc = alpha * acc + pending_pv(hh, qt, qt % 2)
        outs.append(acc[:D_HEAD] / acc[D_HEAD:D_HEAD + 1])
    o_ref[0] = jnp.concatenate(outs, axis=0).T


def _attn_prompt_call(q, k_bf, vt_bf, kmean, bias_tab, n_seq):
    n_tok = q.shape[0]
    t_len = n_tok // n_seq
    n_blk = t_len // MOBA_BLOCK
    n_dist = bias_tab.shape[1]
    q3 = q.reshape(n_seq, t_len, W_BR)
    k4 = k_bf.reshape(n_seq, n_blk, MOBA_BLOCK, W_BR)
    vt4 = vt_bf.reshape(n_seq, n_blk, W_BR, MOBA_BLOCK)
    km3 = kmean.reshape(n_seq, n_blk, W_BR)
    kern = functools.partial(_attn_prompt_kernel, n_blk=n_blk, n_dist=n_dist)
    out = pl.pallas_call(
        kern,
        grid=(n_seq, N_HEAD // 2, n_blk),
        in_specs=[pl.BlockSpec((1, TM, 128), lambda b, hp, t: (b, t, hp)),
                  pl.BlockSpec((1, n_blk, MOBA_BLOCK, 128), lambda b, hp, t: (b, 0, 0, hp)),
                  pl.BlockSpec((1, n_blk, 128, MOBA_BLOCK), lambda b, hp, t: (b, 0, hp, 0)),
                  pl.BlockSpec((1, n_blk, 128), lambda b, hp, t: (b, 0, hp)),
                  pl.BlockSpec((2, n_dist, MOBA_BLOCK, MOBA_BLOCK), lambda b, hp, t: (hp, 0, 0, 0))],
        out_specs=pl.BlockSpec((1, TM, 128), lambda b, hp, t: (b, t, hp)),
        out_shape=jax.ShapeDtypeStruct((n_seq, t_len, W_BR), F32),
        scratch_shapes=[pltpu.VMEM((2, n_blk, TM), F32),
                        pltpu.VMEM((2, 2, MOBA_BLOCK, TM), F32),
                        pltpu.VMEM((2, 2, MOBA_BLOCK, TM), BF16)],
        compiler_params=_cparams(3),
    )(q3, k4, vt4, km3, bias_tab)
    return out.reshape(n_tok, W_BR)


def _stack_heads(x):
    t_len = x.shape[0]
    tiled = jnp.concatenate([x] * N_HEAD, axis=0)
    row = lax.broadcasted_iota(jnp.int32, tiled.shape, 0)
    lane = lax.broadcasted_iota(jnp.int32, tiled.shape, 1)
    return jnp.where(row // t_len == lane // D_HEAD, tiled, 0.0)


def _unstack_heads(y, t_len):
    row = lax.broadcasted_iota(jnp.int32, y.shape, 0)
    lane = lax.broadcasted_iota(jnp.int32, y.shape, 1)
    y = jnp.where(row // t_len == lane // D_HEAD, y, 0.0)
    out = y[0:t_len]
    for h in range(1, N_HEAD):
        out = out + y[h * t_len:(h + 1) * t_len]
    return out


_BQK = (((2,), (2,)), ((0,), (0,)))
_BPV = (((2,), (1,)), ((0,), (0,)))


def _split_heads(x):
    return jnp.stack([x[:, h * D_HEAD:(h + 1) * D_HEAD] for h in range(N_HEAD)], axis=0)


SAMPLE_BLOCKS_PER_STEP = 2
PAGES_PER_BLOCK = 2


def _attn_sample_kernel(pt_ref, q_ref, kn_ref, vn_ref, *refs, n_blk, t_len, page):
    del pt_ref
    n_pg = SAMPLE_BLOCKS_PER_STEP * PAGES_PER_BLOCK
    ck_refs, cv_refs = refs[:n_pg], refs[n_pg:2 * n_pg]
    bias_ref, bown_ref, o_ref, q3_ref, kmean_ref, m_ref, l_ref, oacc_ref = refs[2 * n_pg:]
    step = pl.program_id(1)
    n_row = N_HEAD * t_len

    @pl.when(step == 0)
    def _():
        q3_ref[...] = _split_heads(q_ref[0])
        m_ref[...] = jnp.zeros(m_ref.shape, F32)
        l_ref[...] = jnp.zeros(l_ref.shape, F32)

    def head_major(r0, r1):
        return jnp.stack([jnp.concatenate([r0[0, 0, pl.ds(h, page, stride=N_HEAD), :],
                                           r1[0, 0, pl.ds(h, page, stride=N_HEAD), :]], axis=0)
                          for h in range(N_HEAD)], axis=0)

    qb = q3_ref[...].astype(BF16)
    col = lax.broadcasted_iota(jnp.int32, m_ref.shape, 1)
    m_all, l_all = m_ref[...], l_ref[...]
    for j in range(SAMPLE_BLOCKS_PER_STEP):
        n = step * SAMPLE_BLOCKS_PER_STEP + j
        k3 = head_major(ck_refs[2 * j], ck_refs[2 * j + 1])
        v3 = head_major(cv_refs[2 * j], cv_refs[2 * j + 1])
        kmean = jnp.sum(k3, axis=1, keepdims=True) * (1.0 / MOBA_BLOCK)
        for h in range(N_HEAD):
            kmean_ref[h, pl.ds(n, 1), :] = kmean[h]
        s = lax.dot_general(qb, k3.astype(BF16), _BQK, preferred_element_type=F32)
        s = s.reshape(n_row, MOBA_BLOCK) + bias_ref[n]
        m = jnp.max(s, axis=1, keepdims=True)
        p = jnp.exp(s - m)
        l = jnp.sum(p, axis=1, keepdims=True)
        pv = lax.dot_general(p.astype(BF16).reshape(N_HEAD, t_len, MOBA_BLOCK), v3.astype(BF16), _BPV,
                             preferred_element_type=F32)
        oacc_ref[n] = pv.reshape(n_row, D_HEAD)
        m_all = jnp.where(col == n, m, m_all)
        l_all = jnp.where(col == n, l, l_all)
    m_ref[...] = m_all
    l_ref[...] = l_all

    @pl.when(step == pl.num_programs(1) - 1)
    def _():
        gates = lax.dot_general(qb, kmean_ref[...].astype(BF16), _BQK,
                                preferred_element_type=F32).reshape(n_row, n_blk)
        blk = lax.broadcasted_iota(jnp.int32, gates.shape, 1)
        g = gates
        sel = jnp.zeros(gates.shape, jnp.bool_)
        for _ in range(MOBA_TOPK):
            mx = jnp.max(g, axis=1, keepdims=True)
            idx = jnp.min(jnp.where(g == mx, blk, n_blk), axis=1, keepdims=True)
            hit = blk == idx
            sel = sel | hit
            g = jnp.where(hit, -jnp.inf, g)
        m_all = m_ref[:, 0:n_blk]
        l_all = l_ref[:, 0:n_blk]

        pad = jnp.zeros((128 - t_len, W_BR), F32)
        kown = _split_heads(jnp.concatenate([kn_ref[0], pad], axis=0)).astype(BF16)
        vown = _split_heads(jnp.concatenate([vn_ref[0], pad], axis=0)).astype(BF16)
        s_own = lax.dot_general(qb, kown, _BQK, preferred_element_type=F32).reshape(n_row, 128)
        s_own = s_own + bown_ref[...]
        m_own = jnp.max(s_own, axis=1, keepdims=True)
        m_tot = jnp.maximum(m_own, jnp.max(jnp.where(sel, m_all, -jnp.inf), axis=1, keepdims=True))
        p_own = jnp.exp(s_own - m_tot)
        wgt = jnp.where(sel, jnp.exp(m_all - m_tot), 0.0)
        denom = jnp.sum(p_own, axis=1, keepdims=True) + jnp.sum(wgt * l_all, axis=1, keepdims=True)
        acc = lax.dot_general(p_own.astype(BF16).reshape(N_HEAD, t_len, 128), vown, _BPV,
                              preferred_element_type=F32).reshape(n_row, D_HEAD)
        for i in range(n_blk):
            acc = acc + wgt[:, i:i + 1] * oacc_ref[i]
        out = acc / denom
        o_ref[0] = jnp.concatenate([out[h * t_len:(h + 1) * t_len] for h in range(N_HEAD)], axis=1)


def _attn_sample_call(page_table, q, k_new, v_new, cache_k, cache_v, layer, bias_tab, bias_own):
    n_seq, t_len, _ = q.shape
    n_pages = page_table.shape[1]
    page = cache_k.shape[2] // N_HEAD
    n_blk = n_pages * page // MOBA_BLOCK
    n_row = N_HEAD * t_len
    tok = pl.BlockSpec((1, t_len, W_BR), lambda b, n, pt: (b, 0, 0))

    n_pg = SAMPLE_BLOCKS_PER_STEP * PAGES_PER_BLOCK
    assert page * PAGES_PER_BLOCK == MOBA_BLOCK and n_blk % SAMPLE_BLOCKS_PER_STEP == 0

    def page_spec(j):
        return pl.BlockSpec((1, 1, page * N_HEAD, D_HEAD),
                            lambda b, n, pt: (layer, pt[b, n_pg * n + j], 0, 0))

    pages = [page_spec(j) for j in range(n_pg)]
    kern = functools.partial(_attn_sample_kernel, n_blk=n_blk, t_len=t_len, page=page)
    return pl.pallas_call(
        kern,
        grid_spec=pltpu.PrefetchScalarGridSpec(
            num_scalar_prefetch=1,
            grid=(n_seq, n_blk // SAMPLE_BLOCKS_PER_STEP),
            in_specs=[tok, tok, tok] + pages + pages + [
                      pl.BlockSpec(bias_tab.shape, lambda b, n, pt: (0, 0, 0)),
                      pl.BlockSpec(bias_own.shape, lambda b, n, pt: (0, 0))],
            out_specs=tok,
            scratch_shapes=[pltpu.VMEM((N_HEAD, t_len, D_HEAD), F32),
                            pltpu.VMEM((N_HEAD, n_blk, D_HEAD), F32), pltpu.VMEM((n_row, 128), F32),
                            pltpu.VMEM((n_row, 128), F32), pltpu.VMEM((n_blk, n_row, D_HEAD), F32)]),
        out_shape=jax.ShapeDtypeStruct((n_seq, t_len, W_BR), F32),
        compiler_params=_cparams(2),
    )(page_table, q, k_new, v_new, *([cache_k] * n_pg), *([cache_v] * n_pg), bias_tab, bias_own)


SOLVE_BASE = 4


def _unit_lower_inverse(a, size):
    n = a.shape[-1]
    ri = lax.broadcasted_iota(jnp.int32, (n, n), 0)
    ci = lax.broadcasted_iota(jnp.int32, (n, n), 1)
    eye = (ri == ci).astype(F32)

    def same_block(s):
        return (ri // s) == (ci // s)

    d = jnp.where(same_block(SOLVE_BASE), a, 0.0)
    inv = _mm3(eye - d, eye + _mm3(d, d))
    s = SOLVE_BASE
    while s < size:
        off = jnp.where(same_block(2 * s) & jnp.logical_not(same_block(s)), a, 0.0)
        inv = inv - _dot(_dot(inv, off), inv)
        s *= 2
    return inv


def _gdn_prompt_kernel(q_ref, k_ref, v_ref, beta_ref, gcol_ref, grow_ref, o_ref, s_ref, state_ref,
                       *, chunks):
    @pl.when(pl.program_id(1) == 0)
    def _():
        state_ref[...] = jnp.zeros(state_ref.shape, F32)

    ri = lax.broadcasted_iota(jnp.int32, (DN_CHUNK, DN_CHUNK), 0)
    ci = lax.broadcasted_iota(jnp.int32, (DN_CHUNK, DN_CHUNK), 1)
    tril = ri >= ci
    strict = ri > ci
    lower = tril.astype(F32)

    def chunk(c, carry):
        r0 = pl.multiple_of(c * DN_CHUNK, DN_CHUNK)
        qa = q_ref[0, pl.ds(r0, DN_CHUNK), :] * (D_HEAD ** -0.5)
        ka = k_ref[0, pl.ds(r0, DN_CHUNK), :]
        va = v_ref[0, pl.ds(r0, DN_CHUNK), :]
        beta = beta_ref[0, pl.ds(r0, DN_CHUNK), :]
        gcum_col = jnp.dot(lower, gcol_ref[0, pl.ds(r0, DN_CHUNK), :], precision=HIGHEST,
                           preferred_element_type=F32)
        gcum_row = lax.dot_general(grow_ref[c], lower, _NT, precision=HIGHEST,
                                   preferred_element_type=F32)
        def heads(x):
            return jnp.stack([x[:, h * D_HEAD:(h + 1) * D_HEAD] for h in range(N_HEAD)], axis=0)

        q, k, v = heads(qa), heads(ka), heads(va)
        kt = ka.T.reshape(N_HEAD, D_HEAD, DN_CHUNK)
        gc = jnp.stack([gcum_col[:, h:h + 1] for h in range(N_HEAD)], axis=0)
        gr = jnp.stack([gcum_row[h:h + 1, :] for h in range(N_HEAD)], axis=0)
        b = jnp.stack([beta[:, h:h + 1] for h in range(N_HEAD)], axis=0)
        decay = jnp.where(tril, jnp.exp(jnp.where(tril, gc - gr, 0.0)), 0.0)
        eg = jnp.exp(gc)
        glast = gr[:, :, DN_CHUNK - 1:DN_CHUNK]
        kb = k * b
        a = jnp.where(strict, _dot(kb, kt) * decay, 0.0)
        intra = _dot(q, kt) * decay
        x = _mm3(_unit_lower_inverse(a, DN_CHUNK), jnp.concatenate([v * b, kb * eg], axis=2))
        u, w = x[:, :, :D_HEAD], x[:, :, D_HEAD:]
        s = state_ref[...]
        v_new = u - _dot(w, s)
        o = _dot(q * eg, s) + _dot(intra, v_new)
        state_ref[...] = s * jnp.exp(glast) + _dot(kt * jnp.exp(glast - gr), v_new)
        o_ref[0, pl.ds(r0, DN_CHUNK), :] = jnp.concatenate([o[h] for h in range(N_HEAD)], axis=1)
        return carry

    lax.fori_loop(0, chunks, chunk, 0)
    s_ref[0] = state_ref[...]


def _gdn_prompt_call(qc, kc, vc, beta, gcol, grow, n_seq):
    n_tok = qc.shape[0]
    t_len = n_tok // n_seq
    rows = TM
    chunks = rows // DN_CHUNK
    n_tiles = t_len // rows

    def tok(width):
        return pl.BlockSpec((1, rows, width), lambda b, t: (b, t, 0))

    o, s = pl.pallas_call(
        functools.partial(_gdn_prompt_kernel, chunks=chunks),
        grid=(n_seq, n_tiles),
        in_specs=[tok(W_BR), tok(W_BR), tok(W_BR), tok(N_HEAD), tok(N_HEAD),
                  pl.BlockSpec((chunks, N_HEAD, DN_CHUNK), lambda b, t: (b * n_tiles + t, 0, 0))],
        out_specs=[tok(W_BR), pl.BlockSpec((1, N_HEAD, D_HEAD, D_HEAD), lambda b, t: (b, 0, 0, 0))],
        out_shape=[jax.ShapeDtypeStruct((n_seq, t_len, W_BR), F32),
                   jax.ShapeDtypeStruct((n_seq, N_HEAD, D_HEAD, D_HEAD), F32)],
        scratch_shapes=[pltpu.VMEM((N_HEAD, D_HEAD, D_HEAD), F32)],
        compiler_params=_cparams(2),
    )(qc.reshape(n_seq, t_len, W_BR), kc.reshape(n_seq, t_len, W_BR), vc.reshape(n_seq, t_len, W_BR),
      beta.reshape(n_seq, t_len, N_HEAD), gcol.reshape(n_seq, t_len, N_HEAD), grow)
    return o.reshape(n_tok, W_BR), s


def _gdn_sample_kernel(q_ref, k_ref, v_ref, bcol_ref, gcol_ref, grow_ref, s0_ref, o_ref, s_ref,
                       *, t_len):
    n_row = N_HEAD * t_len
    ri = lax.broadcasted_iota(jnp.int32, (n_row, n_row), 0)
    ci = lax.broadcasted_iota(jnp.int32, (n_row, n_row), 1)
    same = (ri // t_len) == (ci // t_len)
    tril = same & (ri >= ci)
    strict = same & (ri > ci)
    q = _stack_heads(q_ref[0]) * (D_HEAD ** -0.5)
    k = _stack_heads(k_ref[0])
    v = _stack_heads(v_ref[0])
    b = bcol_ref[0]
    g_lane = jnp.broadcast_to(gcol_ref[0], (n_row, 128))
    gc = jnp.dot(tril.astype(F32), g_lane, precision=HIGHEST, preferred_element_type=F32)[:, 0:1]
    gl = jnp.dot(same.astype(F32), g_lane, precision=HIGHEST, preferred_element_type=F32)[:, 0:1]
    g_sub = jnp.broadcast_to(grow_ref[0], (SUBLANE, n_row))
    gr = lax.dot_general(g_sub, tril.astype(F32), _NT, precision=HIGHEST,
                         preferred_element_type=F32)[0:1, :]
    decay = jnp.where(tril, jnp.exp(jnp.where(tril, gc - gr, 0.0)), 0.0)
    eg = jnp.exp(gc)
    kb = k * b
    a = jnp.where(strict, lax.dot_general(kb, k, _NT, preferred_element_type=F32) * decay, 0.0)
    intra = lax.dot_general(q, k, _NT, preferred_element_type=F32) * decay
    vb = v * b
    vb_c = vb[:, 0:D_HEAD]
    for h in range(1, N_HEAD):
        vb_c = vb_c + vb[:, h * D_HEAD:(h + 1) * D_HEAD]
    x = _mm3(_unit_lower_inverse(a, t_len), jnp.concatenate([kb * eg, vb_c], axis=1))
    w, u = x[:, :W_BR], x[:, W_BR:]
    s0 = s0_ref[0]
    v_new = u - jnp.dot(w, s0, preferred_element_type=F32)
    o = jnp.dot(q * eg, s0, preferred_element_type=F32) + jnp.dot(intra, v_new, preferred_element_type=F32)
    er = lax.broadcasted_iota(jnp.int32, (W_BR, n_row), 0)
    ec = lax.broadcasted_iota(jnp.int32, (W_BR, n_row), 1)
    pick = ((er // D_HEAD) == (ec // t_len)) & ((ec % t_len) == t_len - 1)
    gl_state = jnp.dot(pick.astype(F32), jnp.broadcast_to(gc, (n_row, 128)), precision=HIGHEST,
                       preferred_element_type=F32)[:, 0:1]
    s_ref[0] = s0 * jnp.exp(gl_state) + lax.dot_general(
        k * jnp.exp(gl - gc), v_new, _TN, preferred_element_type=F32)
    o_ref[0] = _unstack_heads(jnp.concatenate([o] * N_HEAD, axis=1), t_len)


def _gdn_sample_call(qc, kc, vc, bcol, gcol, grow, s0):
    n_seq, t_len, _ = qc.shape
    n_row = N_HEAD * t_len
    assert t_len >= SOLVE_BASE and t_len & (t_len - 1) == 0
    tok = pl.BlockSpec((1, t_len, W_BR), lambda b: (b, 0, 0))
    colv = pl.BlockSpec((1, n_row, 1), lambda b: (b, 0, 0))
    st = pl.BlockSpec((1, W_BR, D_HEAD), lambda b: (b, 0, 0))
    return pl.pallas_call(
        functools.partial(_gdn_sample_kernel, t_len=t_len),
        grid=(n_seq,),
        in_specs=[tok, tok, tok, colv, colv, pl.BlockSpec((1, 1, n_row), lambda b: (b, 0, 0)), st],
        out_specs=[tok, st],
        out_shape=[jax.ShapeDtypeStruct((n_seq, t_len, W_BR), F32),
                   jax.ShapeDtypeStruct((n_seq, W_BR, D_HEAD), F32)],
        compiler_params=_cparams(1),
    )(qc, kc, vc, bcol, gcol, grow, s0)


def _out_kernel(x_ref, sc_ref, sh_ref, gate_ref, g1_ref, attn_ref, yb_ref, od_ref, og_ref, dng_ref,
                wmg_ref, woa_ref, wob_ref, woc_ref, wo_ref, gsum_ref, o_ref):
    x = x_ref[...]
    h = _mod_norm(x, g1_ref[...], sc_ref[0], sh_ref[0]).astype(BF16)
    mg = _sigmoid(jnp.dot(h, wmg_ref[...], preferred_element_type=F32))
    d = x.shape[1]
    ya = jnp.dot(attn_ref[...].astype(BF16), woa_ref[...], preferred_element_type=F32)
    yb = jnp.dot(yb_ref[...].astype(BF16), wob_ref[...], preferred_element_type=F32)
    od = od_ref[...]
    oc = (od * lax.rsqrt(_group_sumsq(od, gsum_ref[...]) * (1.0 / D_HEAD) + EPS) * dng_ref[...]) \
        * _silu(og_ref[...])
    yc = jnp.dot(oc.astype(BF16), woc_ref[...], preferred_element_type=F32)
    merged = mg[:, :d] * ya + mg[:, d:2 * d] * yb + mg[:, 2 * d:] * yc
    o_ref[...] = x + gate_ref[0] * jnp.dot(merged.astype(BF16), wo_ref[...], preferred_element_type=F32)


def _token_grid(n_tok, n_seq, prompt, d_model):
    if prompt:
        rows = TM
        n_tiles = n_tok // rows
        tiles_per_seq = n_tiles // n_seq
        mod_spec = pl.BlockSpec((1, 1, d_model), lambda t: (t // tiles_per_seq, 0, 0))
    else:
        rows, n_tiles, tiles_per_seq = n_tok, 1, 1
        mod_spec = pl.BlockSpec((1, rows, d_model), lambda t: (0, 0, 0))
    return rows, n_tiles, tiles_per_seq, mod_spec


def _out_call(x, sc, sh, gate, attn, yb, od, og, w, *, prompt, n_seq):
    n_tok, d_model = x.shape
    rows, n_tiles, _, mod_spec = _token_grid(n_tok, n_seq, prompt, d_model)

    def tok(width):
        return pl.BlockSpec((rows, width), lambda t: (t, 0))

    return pl.pallas_call(
        _out_kernel, grid=(n_tiles,),
        in_specs=[tok(d_model), mod_spec, mod_spec, mod_spec, _const_spec((1, d_model)),
                  tok(W_BR), tok(W_BR), tok(W_BR), tok(W_BR), _const_spec((1, W_BR)),
                  _const_spec(w['wmg'].shape), _const_spec(w['woa'].shape), _const_spec(w['wob'].shape),
                  _const_spec(w['woc'].shape), _const_spec(w['wo'].shape), _const_spec((W_BR, W_BR))],
        out_specs=tok(d_model),
        out_shape=jax.ShapeDtypeStruct((n_tok, d_model), F32),
        compiler_params=_cparams(1),
    )(x, sc, sh, gate, w['norm1_g'], attn, yb, od, og, w['dn_norm_g'], w['wmg'], w['woa'], w['wob'],
      w['woc'], w['wo'], w['gsum'])


def _ffn_kernel(x_ref, sc_ref, sh_ref, gate_ref, g2_ref, wg_ref, wu_ref, wd_ref, cw_ref, halo_ref,
                o_ref, tail_ref, xp_ref, *, rows, tiles_per_seq, stride, halo):
    first = (pl.program_id(0) % tiles_per_seq) == 0
    x = x_ref[...]
    h = _mod_norm(x, g2_ref[...], sc_ref[0], sh_ref[0]).astype(BF16)
    pre = jnp.dot(h, wg_ref[...], preferred_element_type=F32)
    hg, tail = _causal_conv(xp_ref, pre, cw_ref, halo_ref, first,
                            width=CONV_F, halo=halo, stride=stride, rows=rows)
    tail_ref[0] = tail
    act = _silu(hg) * jnp.dot(h, wu_ref[...], preferred_element_type=F32)
    o_ref[...] = x + gate_ref[0] * jnp.dot(act.astype(BF16), wd_ref[...], preferred_element_type=F32)


def _ffn_call(x, sc, sh, gate, w, halo_in, *, prompt, n_seq):
    n_tok, d_model = x.shape
    d_ff = w['wgate'].shape[1]
    rows, n_tiles, tiles_per_seq, mod_spec = _token_grid(n_tok, n_seq, prompt, d_model)
    stride = 1 if prompt else n_seq
    halo = SUBLANE if prompt else (CONV_F - 1) * n_seq
    n_grp = n_tiles // tiles_per_seq
    tok = pl.BlockSpec((rows, d_model), lambda t: (t, 0))
    per_seq = pl.BlockSpec((1, halo, d_ff), lambda t: (t // tiles_per_seq, 0, 0))
    kern = functools.partial(_ffn_kernel, rows=rows, tiles_per_seq=tiles_per_seq, stride=stride, halo=halo)
    return pl.pallas_call(
        kern, grid=(n_tiles,),
        in_specs=[tok, mod_spec, mod_spec, mod_spec, _const_spec((1, d_model)),
                  _const_spec(w['wgate'].shape), _const_spec(w['wup'].shape), _const_spec(w['wdown'].shape),
                  _const_spec((CONV_F, d_ff)), per_seq],
        out_specs=[tok, per_seq],
        out_shape=[jax.ShapeDtypeStruct((n_tok, d_model), F32),
                   jax.ShapeDtypeStruct((n_grp, halo, d_ff), F32)],
        scratch_shapes=[pltpu.VMEM((halo + rows, d_ff), F32)],
        compiler_params=_cparams(1),
    )(x, sc, sh, gate, w['norm2_g'], w['wgate'], w['wup'], w['wdown'], w['ffn_conv_w'], halo_in)


def _t5_bucket(rel):
    n = jnp.maximum(rel, 0)
    max_exact = REL_BUCKETS // 2
    nf = jnp.maximum(n, 1).astype(F32)
    large = max_exact + (jnp.log(nf / max_exact) / math.log(REL_MAX_DIST / max_exact)
                         * (REL_BUCKETS - max_exact)).astype(jnp.int32)
    large = jnp.minimum(large, REL_BUCKETS - 1)
    return jnp.where(n < max_exact, n, large)


def _bias_tables(rel_bias, n_blk_prompt, past_len, t_s):
    blk = MOBA_BLOCK
    n_dist = min(n_blk_prompt, -(-(REL_MAX_DIST + blk - 1) // blk) + 1)
    n_rel = n_dist * blk
    vec = rel_bias[_t5_bucket(jnp.arange(n_rel, dtype=jnp.int32))].T
    vec = jnp.concatenate([vec, jnp.full((N_HEAD, blk), NEG, F32)], axis=1)
    n = n_rel + blk
    skew = jnp.tile(vec, (1, blk))[:, :blk * (n - 1)].reshape(N_HEAD, blk, n - 1)
    tab_p = skew[:, :, :n_rel].reshape(N_HEAD, blk, n_dist, blk).transpose(0, 2, 1, 3)
    n_blk_s = past_len // blk
    vec_s = rel_bias[_t5_bucket(jnp.arange(past_len + t_s, dtype=jnp.int32))].T
    tab_s = jnp.stack([jnp.flip(vec_s[:, t + 1:t + 1 + past_len], axis=1) for t in range(t_s)], axis=1)
    tab_s = tab_s.reshape(N_HEAD * t_s, n_blk_s, blk).transpose(1, 0, 2)
    ko2 = jnp.arange(128, dtype=jnp.int32)[None, :]
    rel_o = jnp.arange(t_s, dtype=jnp.int32)[:, None] - ko2
    tab_o = jnp.where(rel_o[..., None] >= 0, rel_bias[_t5_bucket(rel_o)], NEG)
    tab_o = jnp.transpose(tab_o, (2, 0, 1)).reshape(N_HEAD * t_s, 128)
    return tab_p.astype(F32), tab_s.astype(F32), tab_o.astype(F32)


def _layer_weights(l, p, gsum):
    w_in = p['w_in'][l]
    cuts = [0, 3 * W_BR, 6 * W_BR, 10 * W_BR, 10 * W_BR + 2 * N_HEAD]
    wba = w_in[:, cuts[3]:cuts[4]].astype(BF16)
    row = lambda a: a.reshape(1, -1)
    return {
        'wa': w_in[:, cuts[0]:cuts[1]].astype(BF16),
        'wb': w_in[:, cuts[1]:cuts[2]].astype(BF16),
        'wc': w_in[:, cuts[2]:cuts[3]].astype(BF16),
        'wba': wba, 'wbat': wba.T,
        'wmg': w_in[:, cuts[4]:].astype(BF16),
        'norm1_g': row(p['norm1_g'][l]), 'norm2_g': row(p['norm2_g'][l]),
        'qn_g': row(jnp.tile(p['qn_g'][l], N_HEAD)), 'kn_g': row(jnp.tile(p['kn_g'][l], N_HEAD)),
        'dn_norm_g': row(jnp.tile(p['dn_norm_g'][l], N_HEAD)),
        'conv_b_w': p['conv_b_w'][l], 'dn_conv_w': p['dn_conv_w'][l], 'ffn_conv_w': p['ffn_conv_w'][l],
        'a_log_row': row(p['dn_a_log'][l]), 'dt_bias_row': row(p['dn_dt_bias'][l]),
        'a_log_col': p['dn_a_log'][l].reshape(-1, 1), 'dt_bias_col': p['dn_dt_bias'][l].reshape(-1, 1),
        'woa': p['w_oa'][l].astype(BF16), 'wob': p['w_ob'][l].astype(BF16), 'woc': p['w_oc'][l].astype(BF16),
        'wo': p['w_o'][l].astype(BF16),
        'wgate': p['w_gate'][l].astype(BF16), 'wup': p['w_up'][l].astype(BF16),
        'wdown': p['w_down'][l].astype(BF16),
        'gsum': gsum,
    }


def kernel(x_prompt, x_sample, cache_k, cache_v, state_conv_b, state_conv_dn, state_dn, state_conv_ffn,
           page_table, c_prompt, c_sample, rel_bias, w_ada, b_ada, norm1_g, norm2_g, w_in, qn_g, kn_g,
           conv_b_w, dn_conv_w, dn_a_log, dn_dt_bias, dn_norm_g, w_oa, w_ob, w_oc, w_o, w_gate, w_up,
           ffn_conv_w, w_down):
    bp, tp, d_model = x_prompt.shape
    bs, ts, _ = x_sample.shape
    depth = w_in.shape[0]
    d_ff = w_gate.shape[2]
    n_pool, page = cache_k.shape[1], cache_k.shape[2]
    past_len = page_table.shape[1] * page
    assert tp % MOBA_BLOCK == 0 and past_len % MOBA_BLOCK == 0 and MOBA_BLOCK == 2 * page
    assert ts <= 128 and past_len // MOBA_BLOCK >= MOBA_TOPK and (bs * ts) % SUBLANE == 0
    p = dict(w_in=w_in, norm1_g=norm1_g, norm2_g=norm2_g, qn_g=qn_g, kn_g=kn_g, conv_b_w=conv_b_w,
             dn_conv_w=dn_conv_w, dn_a_log=dn_a_log, dn_dt_bias=dn_dt_bias, dn_norm_g=dn_norm_g,
             w_oa=w_oa, w_ob=w_ob, w_oc=w_oc, w_o=w_o, w_gate=w_gate, w_up=w_up,
             ffn_conv_w=ffn_conv_w, w_down=w_down)

    lane = jnp.arange(W_BR) // D_HEAD
    gsum = (lane[:, None] == lane[None, :]).astype(BF16)
    tab_p, tab_s, tab_o = _bias_tables(rel_bias, tp // MOBA_BLOCK, past_len, ts)

    n_c = bp + bs
    c_rows = -(-n_c // SUBLANE) * SUBLANE
    c_all = jnp.concatenate([c_prompt, c_sample, jnp.zeros((c_rows - n_c, d_model), F32)], axis=0)
    mods = _ada_call(c_all, w_ada, b_ada)

    ck = cache_k.reshape(depth, n_pool, page * N_HEAD, D_HEAD)
    cv = cache_v.reshape(depth, n_pool, page * N_HEAD, D_HEAD)

    xp = x_prompt.reshape(bp * tp, d_model)
    xs = jnp.transpose(x_sample, (1, 0, 2)).reshape(ts * bs, d_model)

    def to_tm(a):
        return jnp.transpose(a, (1, 0, 2)).reshape(1, a.shape[1] * bs, a.shape[2])

    def from_tm(a, r):
        return jnp.transpose(a.reshape(r, bs, a.shape[-1]), (1, 0, 2))

    zero_b = jnp.zeros((bp, SUBLANE, W_BR), F32)
    zero_c = jnp.zeros((bp, SUBLANE, 3 * W_BR), F32)
    zero_f = jnp.zeros((bp, SUBLANE, d_ff), F32)

    outs = [[] for _ in range(12)]
    for l in range(depth):
        w = _layer_weights(l, p, gsum)
        mod_p = [m.reshape(bp, 1, d_model) for m in jnp.split(mods[l, :bp], 6, axis=-1)]
        mod_s = [jnp.tile(m, (ts, 1)).reshape(1, ts * bs, d_model)
                 for m in jnp.split(mods[l, bp:bp + bs], 6, axis=-1)]

        sh1, sc1, g1, sh2, sc2, g2 = mod_p
        (q, k, v, yb, tail_b, qc, kc, vc, og, beta, gcol, tail_c, k_bf, vt_bf, kmean, grow) = _in_call(
            xp, sc1, sh1, w, prompt=True, n_seq=bp, halo_b_in=zero_b, halo_c_in=zero_c)
        attn = _attn_prompt_call(q, k_bf, vt_bf, kmean, tab_p, bp)
        od, s_p = _gdn_prompt_call(qc, kc, vc, beta, gcol, grow, bp)
        xp = _out_call(xp, sc1, sh1, g1, attn, yb, od, og, w, prompt=True, n_seq=bp)
        xp, tail_f = _ffn_call(xp, sc2, sh2, g2, w, zero_f, prompt=True, n_seq=bp)
        outs[0].append(k.reshape(bp, tp, N_HEAD, D_HEAD))
        outs[1].append(v.reshape(bp, tp, N_HEAD, D_HEAD))
        outs[4].append(tail_b[:, SUBLANE - (CONV_B - 1):])
        outs[6].append(tail_c[:, SUBLANE - (CONV_C - 1):])
        outs[8].append(s_p)
        outs[10].append(tail_f[:, SUBLANE - (CONV_F - 1):])

        sh1, sc1, g1, sh2, sc2, g2 = mod_s
        (q, k, v, yb, tail_b, qc, kc, vc, og, beta, gcol, tail_c) = _in_call(
            xs, sc1, sh1, w, prompt=False, n_seq=bs,
            halo_b_in=to_tm(state_conv_b[l]), halo_c_in=to_tm(state_conv_dn[l]))
        q_b, k_b, v_b = from_tm(q, ts), from_tm(k, ts), from_tm(v, ts)
        attn = _attn_sample_call(page_table, q_b, k_b, v_b, ck, cv, l, tab_s, tab_o)
        beta_ht = jnp.transpose(beta.reshape(ts, bs, N_HEAD), (1, 2, 0)).reshape(bs, N_HEAD * ts)
        g_ht = jnp.transpose(gcol.reshape(ts, bs, N_HEAD), (1, 2, 0)).reshape(bs, N_HEAD * ts)
        od, s_s = _gdn_sample_call(from_tm(qc, ts), from_tm(kc, ts), from_tm(vc, ts),
                                   beta_ht[:, :, None], g_ht[:, :, None], g_ht[:, None, :],
                                   state_dn[l].reshape(bs, W_BR, D_HEAD))
        attn_tm = jnp.transpose(attn, (1, 0, 2)).reshape(ts * bs, W_BR)
        od_tm = jnp.transpose(od, (1, 0, 2)).reshape(ts * bs, W_BR)
        xs = _out_call(xs, sc1, sh1, g1, attn_tm, yb, od_tm, og, w, prompt=False, n_seq=bs)
        xs, tail_f = _ffn_call(xs, sc2, sh2, g2, w, to_tm(state_conv_ffn[l]), prompt=False, n_seq=bs)
        outs[2].append(k_b.reshape(bs, ts, N_HEAD, D_HEAD))
        outs[3].append(v_b.reshape(bs, ts, N_HEAD, D_HEAD))
        outs[5].append(from_tm(tail_b[0], CONV_B - 1))
        outs[7].append(from_tm(tail_c[0], CONV_C - 1))
        outs[9].append(s_s.reshape(bs, N_HEAD, D_HEAD, D_HEAD))
        outs[11].append(from_tm(tail_f[0], CONV_F - 1))

    y_p = xp.reshape(bp, tp, d_model)
    y_s = from_tm(xs, ts)
    return (y_p, y_s) + tuple(jnp.stack(o) for o in outs)
```

```python
import functools
import math

import jax
import jax.numpy as jnp
from jax import lax
from jax.experimental import pallas as pl
from jax.experimental.pallas import tpu as pltpu

F32 = jnp.float32
BF16 = jnp.bfloat16

N_HEAD = 8
D_HEAD = 64
W_BR = N_HEAD * D_HEAD
MOBA_BLOCK = 256
MOBA_TOPK = 3
CONV_B = 3
CONV_C = 4
CONV_F = 3
DN_CHUNK = 64
REL_BUCKETS = 32
REL_MAX_DIST = 2048
N_BRANCH = 3
EPS = 1e-6
NEG = -1e30

TM = MOBA_BLOCK
SUBLANE = 8
VMEM_LIMIT = 56 * 1024 * 1024
HIGHEST = lax.Precision.HIGHEST

_NT = (((1,), (1,)), ((), ()))
_TN = (((0,), (0,)), ((), ()))


def _const_spec(shape):
    nd = len(shape)
    return pl.BlockSpec(shape, lambda *_: (0,) * nd, pipeline_mode=pl.Buffered(1))


def _cparams(n_grid):
    return pltpu.CompilerParams(dimension_semantics=("arbitrary",) * n_grid,
                                vmem_limit_bytes=VMEM_LIMIT)


def _sigmoid(x):
    return 1.0 / (1.0 + jnp.exp(-x))


def _silu(x):
    return x * _sigmoid(x)


def _softplus(x):
    return jnp.maximum(x, 0.0) + jnp.log1p(jnp.exp(-jnp.abs(x)))


def _split_dot(a, b_bf16):
    hi = a.astype(BF16)
    lo = (a - hi.astype(F32)).astype(BF16)
    return (jnp.dot(hi, b_bf16, preferred_element_type=F32)
            + jnp.dot(lo, b_bf16, preferred_element_type=F32))


def _dot(a, b):
    if a.ndim == 3:
        return lax.dot_general(a, b, (((2,), (1,)), ((0,), (0,))), preferred_element_type=F32)
    return jnp.dot(a, b, preferred_element_type=F32)


def _mm3(a, b):
    ah = a.astype(BF16)
    al = (a - ah.astype(F32)).astype(BF16)
    bh = b.astype(BF16)
    bl = (b - bh.astype(F32)).astype(BF16)
    return _dot(ah, bh) + _dot(ah, bl) + _dot(al, bh)


def _group_sumsq(y, gsum_bf16):
    return _split_dot(y * y, gsum_bf16)


def _mod_norm(x, gain, sc, sh):
    ms = jnp.mean(x * x, axis=-1, keepdims=True)
    return (x * lax.rsqrt(ms + EPS) * gain) * (1.0 + sc) + sh


def _causal_conv(xp_ref, cur, w_ref, halo_ref, first, *, width, halo, stride, rows):
    @pl.when(first)
    def _():
        xp_ref[0:halo, :] = halo_ref[0]

    xp_ref[halo:halo + rows, :] = cur
    off = halo - (width - 1) * stride
    y = xp_ref[off:off + rows, :] * w_ref[0:1, :]
    for i in range(1, width):
        off = halo - (width - 1 - i) * stride
        y = y + xp_ref[off:off + rows, :] * w_ref[i:i + 1, :]
    tail = xp_ref[rows:rows + halo, :]
    xp_ref[0:halo, :] = tail
    return y, tail


def _ada_kernel(c_ref, w_ref, b_ref, o_ref):
    c = _silu(c_ref[...]).astype(BF16)
    o_ref[0] = jnp.dot(c, w_ref[0].astype(BF16), preferred_element_type=F32) + b_ref[0]


def _ada_call(c_all, w_ada, b_ada):
    depth, d_model, n_mod = w_ada.shape
    rows = c_all.shape[0]
    bn = 1536
    return pl.pallas_call(
        _ada_kernel,
        grid=(depth, n_mod // bn),
        in_specs=[pl.BlockSpec((rows, d_model), lambda l, j: (0, 0)),
                  pl.BlockSpec((1, d_model, bn), lambda l, j: (l, 0, j)),
                  pl.BlockSpec((1, 1, bn), lambda l, j: (l, 0, j))],
        out_specs=pl.BlockSpec((1, rows, bn), lambda l, j: (l, 0, j)),
        out_shape=jax.ShapeDtypeStruct((depth, rows, n_mod), F32),
        compiler_params=_cparams(2),
    )(c_all, w_ada, b_ada.reshape(depth, 1, n_mod))


def _in_kernel(*refs, prompt, rows, tiles_per_seq, stride, halo_b, halo_c):
    (x_ref, sc_ref, sh_ref, g1_ref, wa_ref, wb_ref, wc_ref, wba_ref, wbat_ref, qn_ref, kn_ref,
     cbw_ref, cdw_ref, alog_ref, dtb_ref, alogc_ref, dtbc_ref, gsum_ref, hb_ref, hc_ref) = refs[:20]
    outs = refs[20:-2]
    xpb_ref, xpc_ref = refs[-2:]
    if prompt:
        (q_ref, k_ref, v_ref, yb_ref, tailb_ref, qc_ref, kc_ref, vc_ref, og_ref, beta_ref,
         gcol_ref, tailc_ref, kbf_ref, vt_ref, kmean_ref, grow_ref) = outs
    else:
        (q_ref, k_ref, v_ref, yb_ref, tailb_ref, qc_ref, kc_ref, vc_ref, og_ref, beta_ref,
         gcol_ref, tailc_ref) = outs

    first = (pl.program_id(0) % tiles_per_seq) == 0
    gsum = gsum_ref[...]
    h = _mod_norm(x_ref[...], g1_ref[...], sc_ref[0], sh_ref[0])
    hb16 = h.astype(BF16)

    za = jnp.dot(hb16, wa_ref[...], preferred_element_type=F32)
    qa, ka, va = za[:, :W_BR], za[:, W_BR:2 * W_BR], za[:, 2 * W_BR:]
    q = (qa * lax.rsqrt(_group_sumsq(qa, gsum) * (1.0 / D_HEAD) + EPS) * qn_ref[...]) * (D_HEAD ** -0.5)
    k = ka * lax.rsqrt(_group_sumsq(ka, gsum) * (1.0 / D_HEAD) + EPS) * kn_ref[...]
    q_ref[...] = q
    k_ref[...] = k
    v_ref[...] = va
    if prompt:
        kbf_ref[0] = k.astype(BF16)
        vt_ref[0] = va.T.astype(BF16)
        kmean_ref[0] = jnp.sum(k, axis=0, keepdims=True) * (1.0 / MOBA_BLOCK)

    zb = jnp.dot(hb16, wb_ref[...], preferred_element_type=F32)
    hb, bg, cg = zb[:, :W_BR], zb[:, W_BR:2 * W_BR], zb[:, 2 * W_BR:]
    uc, tail_b = _causal_conv(xpb_ref, cg * hb, cbw_ref, hb_ref, first,
                              width=CONV_B, halo=halo_b, stride=stride, rows=rows)
    yb_ref[...] = bg * uc
    tailb_ref[0] = tail_b

    zc = jnp.dot(hb16, wc_ref[...], preferred_element_type=F32)
    qkv, tail_c = _causal_conv(xpc_ref, zc[:, :3 * W_BR], cdw_ref, hc_ref, first,
                               width=CONV_C, halo=halo_c, stride=stride, rows=rows)
    tailc_ref[0] = tail_c
    og_ref[...] = zc[:, 3 * W_BR:]
    qkv = _silu(qkv)
    qc, kc, vc = qkv[:, :W_BR], qkv[:, W_BR:2 * W_BR], qkv[:, 2 * W_BR:]
    qc_ref[...] = qc * lax.rsqrt(_group_sumsq(qc, gsum) + EPS)
    kc_ref[...] = kc * lax.rsqrt(_group_sumsq(kc, gsum) + EPS)
    vc_ref[...] = vc
    zba = jnp.dot(hb16, wba_ref[...], preferred_element_type=F32)
    beta_ref[...] = _sigmoid(zba[:, :N_HEAD])
    gcol_ref[...] = -jnp.exp(alog_ref[...]) * _softplus(zba[:, N_HEAD:] + dtb_ref[...])
    if prompt:
        zbat = lax.dot_general(wbat_ref[...], hb16, _NT, preferred_element_type=F32)
        g_row = -jnp.exp(alogc_ref[...]) * _softplus(zbat[N_HEAD:, :] + dtbc_ref[...])
        for j in range(rows // DN_CHUNK):
            grow_ref[j] = g_row[:, j * DN_CHUNK:(j + 1) * DN_CHUNK]


def _in_call(x, sc, sh, w, *, prompt, n_seq, halo_b_in, halo_c_in):
    n_tok, d_model = x.shape
    if prompt:
        rows, stride = TM, 1
        n_tiles = n_tok // rows
        tiles_per_seq = n_tiles // n_seq
        halo_b = halo_c = SUBLANE
        mod_spec = pl.BlockSpec((1, 1, d_model), lambda t: (t // tiles_per_seq, 0, 0))
    else:
        rows, stride = n_tok, n_seq
        n_tiles, tiles_per_seq = 1, 1
        halo_b, halo_c = (CONV_B - 1) * n_seq, (CONV_C - 1) * n_seq
        mod_spec = pl.BlockSpec((1, rows, d_model), lambda t: (0, 0, 0))
    n_grp = n_tiles // tiles_per_seq

    def tok(width):
        return pl.BlockSpec((rows, width), lambda t: (t, 0))

    def per_seq(r, width):
        return pl.BlockSpec((1, r, width), lambda t: (t // tiles_per_seq, 0, 0))

    in_specs = [tok(d_model), mod_spec, mod_spec, _const_spec((1, d_model)),
                _const_spec(w['wa'].shape), _const_spec(w['wb'].shape), _const_spec(w['wc'].shape),
                _const_spec(w['wba'].shape), _const_spec(w['wbat'].shape),
                _const_spec((1, W_BR)), _const_spec((1, W_BR)),
                _const_spec((CONV_B, W_BR)), _const_spec((CONV_C, 3 * W_BR)),
                _const_spec((1, N_HEAD)), _const_spec((1, N_HEAD)),
                _const_spec((N_HEAD, 1)), _const_spec((N_HEAD, 1)),
                _const_spec((W_BR, W_BR)),
                per_seq(halo_b, W_BR), per_seq(halo_c, 3 * W_BR)]
    out_specs = [tok(W_BR), tok(W_BR), tok(W_BR), tok(W_BR), per_seq(halo_b, W_BR),
                 tok(W_BR), tok(W_BR), tok(W_BR), tok(W_BR), tok(N_HEAD), tok(N_HEAD),
                 per_seq(halo_c, 3 * W_BR)]
    out_shape = [jax.ShapeDtypeStruct((n_tok, W_BR), F32)] * 4 + [
        jax.ShapeDtypeStruct((n_grp, halo_b, W_BR), F32)] + [
        jax.ShapeDtypeStruct((n_tok, W_BR), F32)] * 4 + [
        jax.ShapeDtypeStruct((n_tok, N_HEAD), F32)] * 2 + [
        jax.ShapeDtypeStruct((n_grp, halo_c, 3 * W_BR), F32)]
    if prompt:
        cpt = rows // DN_CHUNK
        out_specs += [pl.BlockSpec((1, rows, W_BR), lambda t: (t, 0, 0)),
                      pl.BlockSpec((1, W_BR, rows), lambda t: (t, 0, 0)),
                      pl.BlockSpec((1, 1, W_BR), lambda t: (t, 0, 0)),
                      pl.BlockSpec((cpt, N_HEAD, DN_CHUNK), lambda t: (t, 0, 0))]
        out_shape += [jax.ShapeDtypeStruct((n_tiles, rows, W_BR), BF16),
                      jax.ShapeDtypeStruct((n_tiles, W_BR, rows), BF16),
                      jax.ShapeDtypeStruct((n_tiles, 1, W_BR), F32),
                      jax.ShapeDtypeStruct((n_tiles * cpt, N_HEAD, DN_CHUNK), F32)]
    kern = functools.partial(_in_kernel, prompt=prompt, rows=rows, tiles_per_seq=tiles_per_seq,
                             stride=stride, halo_b=halo_b, halo_c=halo_c)
    return pl.pallas_call(
        kern, grid=(n_tiles,), in_specs=in_specs, out_specs=out_specs, out_shape=out_shape,
        scratch_shapes=[pltpu.VMEM((halo_b + rows, W_BR), F32),
                        pltpu.VMEM((halo_c + rows, 3 * W_BR), F32)],
        compiler_params=_cparams(1),
    )(x, sc, sh, w['norm1_g'], w['wa'], w['wb'], w['wc'], w['wba'], w['wbat'], w['qn_g'], w['kn_g'],
      w['conv_b_w'], w['dn_conv_w'], w['a_log_row'], w['dt_bias_row'], w['a_log_col'],
      w['dt_bias_col'], w['gsum'], halo_b_in, halo_c_in)


def _gate_scores(a, b):
    return lax.dot_general(a.astype(BF16), b.astype(BF16), _NT, preferred_element_type=F32)


def _top3_mask_cols(gates, own, n_blk):
    blk = lax.broadcasted_iota(jnp.int32, gates.shape, 0)
    g = jnp.where(blk < own, gates, NEG)
    mask = jnp.full(gates.shape, NEG, F32)
    for _ in range(MOBA_TOPK):
        mx = jnp.max(g, axis=0, keepdims=True)
        idx = jnp.min(jnp.where(g == mx, blk, n_blk), axis=0, keepdims=True)
        hit = blk == idx
        mask = jnp.where(hit & (idx < own), 0.0, mask)
        g = jnp.where(hit, -jnp.inf, g)
    return mask


def _attn_prompt_kernel(q_ref, k_ref, vt_ref, kmean_ref, bias_ref, o_ref, sel_ref, s_ref, p_ref,
                        *, n_blk, n_dist):
    qt = pl.program_id(2)
    qf = q_ref[0]
    lane = lax.broadcasted_iota(jnp.int32, qf.shape, 1)
    ones = jnp.ones((16, MOBA_BLOCK), BF16)
    qbs = []
    for hh in range(2):
        qh = jnp.where(lane // D_HEAD == hh, qf, 0.0)
        sel_ref[hh] = _top3_mask_cols(_gate_scores(kmean_ref[0], qh), qt, n_blk)
        qbs.append(qh.astype(BF16))

    def lhs_v(hh, n):
        return jnp.concatenate([vt_ref[0, n, hh * D_HEAD:(hh + 1) * D_HEAD, :], ones], axis=0)

    def scores(hh, n):
        return lax.dot_general(k_ref[0, n], qbs[hh], _NT, preferred_element_type=F32)

    def masked_scores(hh, n):
        d = jnp.minimum(qt - n, n_dist - 1)
        return scores(hh, n) + bias_ref[hh, d] + sel_ref[hh, pl.ds(n, 1), :]

    carry = []
    for hh in range(2):
        s = scores(hh, qt) + bias_ref[hh, 0]
        m = jnp.max(s, axis=0, keepdims=True)
        p_ref[1, hh] = jnp.exp(s - m).astype(BF16)
        carry.append((m, jnp.ones_like(m), jnp.zeros((D_HEAD + 16, TM), F32)))
        s_ref[0, hh] = masked_scores(hh, 0)
    carry = tuple(carry)

    def pending_pv(hh, n, slot):
        v_blk = jnp.where(n == 0, qt, n - 1)
        return jnp.dot(lhs_v(hh, v_blk), p_ref[1 - slot, hh], preferred_element_type=F32)

    def step(n, carry):
        slot = n % 2
        pv = [pending_pv(hh, n, slot) for hh in range(2)]
        nxt = [masked_scores(hh, n + 1) for hh in range(2)]
        out = []
        for hh in range(2):
            m, alpha, acc = carry[hh]
            s = s_ref[slot, hh]
            m_new = jnp.maximum(m, jnp.max(s, axis=0, keepdims=True))
            p_ref[slot, hh] = jnp.exp(s - m_new).astype(BF16)
            out.append((m_new, jnp.exp(m - m_new), alpha * acc + pv[hh]))
        for hh in range(2):
            s_ref[1 - slot, hh] = nxt[hh]
        return tuple(out)

    carry = lax.fori_loop(0, qt, step, carry)
    outs = []
    for hh in range(2):
        _, alpha, acc = carry[hh]
        acc = alpha * acc + pending_pv(hh, qt, qt % 2)
        outs.append(acc[:D_HEAD] / acc[D_HEAD:D_HEAD + 1])
    o_ref[0] = jnp.concatenate(outs, axis=0).T


def _attn_prompt_call(q, k_bf, vt_bf, kmean, bias_tab, n_seq):
    n_tok = q.shape[0]
    t_len = n_tok // n_seq
    n_blk = t_len // MOBA_BLOCK
    n_dist = bias_tab.shape[1]
    q3 = q.reshape(n_seq, t_len, W_BR)
    k4 = k_bf.reshape(n_seq, n_blk, MOBA_BLOCK, W_BR)
    vt4 = vt_bf.reshape(n_seq, n_blk, W_BR, MOBA_BLOCK)
    km3 = kmean.reshape(n_seq, n_blk, W_BR)
    kern = functools.partial(_attn_prompt_kernel, n_blk=n_blk, n_dist=n_dist)
    out = pl.pallas_call(
        kern,
        grid=(n_seq, N_HEAD // 2, n_blk),
        in_specs=[pl.BlockSpec((1, TM, 128), lambda b, hp, t: (b, t, hp)),
                  pl.BlockSpec((1, n_blk, MOBA_BLOCK, 128), lambda b, hp, t: (b, 0, 0, hp)),
                  pl.BlockSpec((1, n_blk, 128, MOBA_BLOCK), lambda b, hp, t: (b, 0, hp, 0)),
                  pl.BlockSpec((1, n_blk, 128), lambda b, hp, t: (b, 0, hp)),
                  pl.BlockSpec((2, n_dist, MOBA_BLOCK, MOBA_BLOCK), lambda b, hp, t: (hp, 0, 0, 0))],
        out_specs=pl.BlockSpec((1, TM, 128), lambda b, hp, t: (b, t, hp)),
        out_shape=jax.ShapeDtypeStruct((n_seq, t_len, W_BR), F32),
        scratch_shapes=[pltpu.VMEM((2, n_blk, TM), F32),
                        pltpu.VMEM((2, 2, MOBA_BLOCK, TM), F32),
                        pltpu.VMEM((2, 2, MOBA_BLOCK, TM), BF16)],
        compiler_params=_cparams(3),
    )(q3, k4, vt4, km3, bias_tab)
    return out.reshape(n_tok, W_BR)


def _stack_heads(x):
    t_len = x.shape[0]
    tiled = jnp.concatenate([x] * N_HEAD, axis=0)
    row = lax.broadcasted_iota(jnp.int32, tiled.shape, 0)
    lane = lax.broadcasted_iota(jnp.int32, tiled.shape, 1)
    return jnp.where(row // t_len == lane // D_HEAD, tiled, 0.0)


def _unstack_heads(y, t_len):
    row = lax.broadcasted_iota(jnp.int32, y.shape, 0)
    lane = lax.broadcasted_iota(jnp.int32, y.shape, 1)
    y = jnp.where(row // t_len == lane // D_HEAD, y, 0.0)
    out = y[0:t_len]
    for h in range(1, N_HEAD):
        out = out + y[h * t_len:(h + 1) * t_len]
    return out


_BQK = (((2,), (2,)), ((0,), (0,)))
_BPV = (((2,), (1,)), ((0,), (0,)))


def _split_heads(x):
    return jnp.stack([x[:, h * D_HEAD:(h + 1) * D_HEAD] for h in range(N_HEAD)], axis=0)


SAMPLE_BLOCKS_PER_STEP = 2
PAGES_PER_BLOCK = 2


def _attn_sample_kernel(pt_ref, q_ref, kn_ref, vn_ref, *refs, n_blk, t_len, page):
    del pt_ref
    n_pg = SAMPLE_BLOCKS_PER_STEP * PAGES_PER_BLOCK
    ck_refs, cv_refs = refs[:n_pg], refs[n_pg:2 * n_pg]
    bias_ref, bown_ref, o_ref, q3_ref, kmean_ref, m_ref, l_ref, oacc_ref = refs[2 * n_pg:]
    step = pl.program_id(1)
    n_row = N_HEAD * t_len

    @pl.when(step == 0)
    def _():
        q3_ref[...] = _split_heads(q_ref[0])
        m_ref[...] = jnp.zeros(m_ref.shape, F32)
        l_ref[...] = jnp.zeros(l_ref.shape, F32)
        kmean_ref[...] = jnp.zeros(kmean_ref.shape, F32)

    qb = q3_ref[...].astype(BF16)
    col = lax.broadcasted_iota(jnp.int32, m_ref.shape, 1)
    kcol = lax.broadcasted_iota(jnp.int32, kmean_ref.shape, 2)
    m_all, l_all, kmean_all = m_ref[...], l_ref[...], kmean_ref[...]
    for j in range(SAMPLE_BLOCKS_PER_STEP):
        n = step * SAMPLE_BLOCKS_PER_STEP + j
        s_pg, ksum = [], None
        for i in range(PAGES_PER_BLOCK):
            kt = ck_refs[PAGES_PER_BLOCK * j + i][0, 0]
            part = jnp.sum(kt, axis=2, keepdims=True)
            ksum = part if ksum is None else ksum + part
            s_pg.append(_dot(qb, kt.astype(BF16)))
        kmean_all = jnp.where(kcol == n, ksum * (1.0 / MOBA_BLOCK), kmean_all)
        s = jnp.concatenate(s_pg, axis=2).reshape(n_row, MOBA_BLOCK) + bias_ref[n]
        m = jnp.max(s, axis=1, keepdims=True)
        p = jnp.exp(s - m)
        l = jnp.sum(p, axis=1, keepdims=True)
        pb = p.astype(BF16)
        pv = None
        for i in range(PAGES_PER_BLOCK):
            vt = cv_refs[PAGES_PER_BLOCK * j + i][0, 0].astype(BF16)
            part = lax.dot_general(pb[:, i * page:(i + 1) * page].reshape(N_HEAD, t_len, page), vt, _BQK,
                                   preferred_element_type=F32)
            pv = part if pv is None else pv + part
        oacc_ref[n] = pv.reshape(n_row, D_HEAD)
        m_all = jnp.where(col == n, m, m_all)
        l_all = jnp.where(col == n, l, l_all)
    m_ref[...] = m_all
    l_ref[...] = l_all
    kmean_ref[...] = kmean_all

    @pl.when(step == pl.num_programs(1) - 1)
    def _():
        gates = _dot(qb, kmean_all.astype(BF16))[:, :, 0:n_blk].reshape(n_row, n_blk)
        blk = lax.broadcasted_iota(jnp.int32, gates.shape, 1)
        g = gates
        sel = jnp.zeros(gates.shape, jnp.bool_)
        for _ in range(MOBA_TOPK):
            mx = jnp.max(g, axis=1, keepdims=True)
            idx = jnp.min(jnp.where(g == mx, blk, n_blk), axis=1, keepdims=True)
            hit = blk == idx
            sel = sel | hit
            g = jnp.where(hit, -jnp.inf, g)
        m_all = m_ref[:, 0:n_blk]
        l_all = l_ref[:, 0:n_blk]

        pad = jnp.zeros((128 - t_len, W_BR), F32)
        kown = _split_heads(jnp.concatenate([kn_ref[0], pad], axis=0)).astype(BF16)
        vown = _split_heads(jnp.concatenate([vn_ref[0], pad], axis=0)).astype(BF16)
        s_own = lax.dot_general(qb, kown, _BQK, preferred_element_type=F32).reshape(n_row, 128)
        s_own = s_own + bown_ref[...]
        m_own = jnp.max(s_own, axis=1, keepdims=True)
        m_tot = jnp.maximum(m_own, jnp.max(jnp.where(sel, m_all, -jnp.inf), axis=1, keepdims=True))
        p_own = jnp.exp(s_own - m_tot)
        wgt = jnp.where(sel, jnp.exp(m_all - m_tot), 0.0)
        denom = jnp.sum(p_own, axis=1, keepdims=True) + jnp.sum(wgt * l_all, axis=1, keepdims=True)
        acc = lax.dot_general(p_own.astype(BF16).reshape(N_HEAD, t_len, 128), vown, _BPV,
                              preferred_element_type=F32).reshape(n_row, D_HEAD)
        for i in range(n_blk):
            acc = acc + wgt[:, i:i + 1] * oacc_ref[i]
        out = acc / denom
        o_ref[0] = jnp.concatenate([out[h * t_len:(h + 1) * t_len] for h in range(N_HEAD)], axis=1)


def _attn_sample_call(page_table, q, k_new, v_new, cache_k, cache_v, layer, bias_tab, bias_own):
    n_seq, t_len, _ = q.shape
    n_pages = page_table.shape[1]
    page = cache_k.shape[4]
    n_blk = n_pages * page // MOBA_BLOCK
    n_row = N_HEAD * t_len
    tok = pl.BlockSpec((1, t_len, W_BR), lambda b, n, pt: (b, 0, 0))

    n_pg = SAMPLE_BLOCKS_PER_STEP * PAGES_PER_BLOCK
    assert page * PAGES_PER_BLOCK == MOBA_BLOCK and n_blk % SAMPLE_BLOCKS_PER_STEP == 0 and n_blk <= 128

    def page_spec(j):
        return pl.BlockSpec((1, 1, N_HEAD, D_HEAD, page),
                            lambda b, n, pt: (layer, pt[b, n_pg * n + j], 0, 0, 0))

    pages = [page_spec(j) for j in range(n_pg)]
    kern = functools.partial(_attn_sample_kernel, n_blk=n_blk, t_len=t_len, page=page)
    return pl.pallas_call(
        kern,
        grid_spec=pltpu.PrefetchScalarGridSpec(
            num_scalar_prefetch=1,
            grid=(n_seq, n_blk // SAMPLE_BLOCKS_PER_STEP),
            in_specs=[tok, tok, tok] + pages + pages + [
                      pl.BlockSpec(bias_tab.shape, lambda b, n, pt: (0, 0, 0)),
                      pl.BlockSpec(bias_own.shape, lambda b, n, pt: (0, 0))],
            out_specs=tok,
            scratch_shapes=[pltpu.VMEM((N_HEAD, t_len, D_HEAD), F32),
                            pltpu.VMEM((N_HEAD, D_HEAD, 128), F32), pltpu.VMEM((n_row, 128), F32),
                            pltpu.VMEM((n_row, 128), F32), pltpu.VMEM((n_blk, n_row, D_HEAD), F32)]),
        out_shape=jax.ShapeDtypeStruct((n_seq, t_len, W_BR), F32),
        compiler_params=_cparams(2),
    )(page_table, q, k_new, v_new, *([cache_k] * n_pg), *([cache_v] * n_pg), bias_tab, bias_own)


SOLVE_BASE = 4


def _unit_lower_inverse(a, size):
    n = a.shape[-1]
    ri = lax.broadcasted_iota(jnp.int32, (n, n), 0)
    ci = lax.broadcasted_iota(jnp.int32, (n, n), 1)
    eye = (ri == ci).astype(F32)

    def same_block(s):
        return (ri // s) == (ci // s)

    d = jnp.where(same_block(SOLVE_BASE), a, 0.0)
    inv = _mm3(eye - d, eye + _mm3(d, d))
    s = SOLVE_BASE
    while s < size:
        off = jnp.where(same_block(2 * s) & jnp.logical_not(same_block(s)), a, 0.0)
        inv = inv - _dot(_dot(inv, off), inv)
        s *= 2
    return inv


def _gdn_prompt_kernel(q_ref, k_ref, v_ref, beta_ref, gcol_ref, grow_ref, o_ref, s_ref, state_ref,
                       *, chunks):
    @pl.when(pl.program_id(1) == 0)
    def _():
        state_ref[...] = jnp.zeros(state_ref.shape, F32)

    ri = lax.broadcasted_iota(jnp.int32, (DN_CHUNK, DN_CHUNK), 0)
    ci = lax.broadcasted_iota(jnp.int32, (DN_CHUNK, DN_CHUNK), 1)
    tril = ri >= ci
    strict = ri > ci
    lower = tril.astype(F32)

    def chunk(c, carry):
        r0 = pl.multiple_of(c * DN_CHUNK, DN_CHUNK)
        qa = q_ref[0, pl.ds(r0, DN_CHUNK), :] * (D_HEAD ** -0.5)
        ka = k_ref[0, pl.ds(r0, DN_CHUNK), :]
        va = v_ref[0, pl.ds(r0, DN_CHUNK), :]
        beta = beta_ref[0, pl.ds(r0, DN_CHUNK), :]
        gcum_col = jnp.dot(lower, gcol_ref[0, pl.ds(r0, DN_CHUNK), :], precision=HIGHEST,
                           preferred_element_type=F32)
        gcum_row = lax.dot_general(grow_ref[c], lower, _NT, precision=HIGHEST,
                                   preferred_element_type=F32)
        def heads(x):
            return jnp.stack([x[:, h * D_HEAD:(h + 1) * D_HEAD] for h in range(N_HEAD)], axis=0)

        q, k, v = heads(qa), heads(ka), heads(va)
        kt = ka.T.reshape(N_HEAD, D_HEAD, DN_CHUNK)
        gc = jnp.stack([gcum_col[:, h:h + 1] for h in range(N_HEAD)], axis=0)
        gr = jnp.stack([gcum_row[h:h + 1, :] for h in range(N_HEAD)], axis=0)
        b = jnp.stack([beta[:, h:h + 1] for h in range(N_HEAD)], axis=0)
        decay = jnp.where(tril, jnp.exp(jnp.where(tril, gc - gr, 0.0)), 0.0)
        eg = jnp.exp(gc)
        glast = gr[:, :, DN_CHUNK - 1:DN_CHUNK]
        kb = k * b
        a = jnp.where(strict, _dot(kb, kt) * decay, 0.0)
        intra = _dot(q, kt) * decay
        x = _mm3(_unit_lower_inverse(a, DN_CHUNK), jnp.concatenate([v * b, kb * eg], axis=2))
        u, w = x[:, :, :D_HEAD], x[:, :, D_HEAD:]
        s = state_ref[...]
        v_new = u - _dot(w, s)
        o = _dot(q * eg, s) + _dot(intra, v_new)
        state_ref[...] = s * jnp.exp(glast) + _dot(kt * jnp.exp(glast - gr), v_new)
        o_ref[0, pl.ds(r0, DN_CHUNK), :] = jnp.concatenate([o[h] for h in range(N_HEAD)], axis=1)
        return carry

    lax.fori_loop(0, chunks, chunk, 0)
    s_ref[0] = state_ref[...]


def _gdn_prompt_call(qc, kc, vc, beta, gcol, grow, n_seq):
    n_tok = qc.shape[0]
    t_len = n_tok // n_seq
    rows = TM
    chunks = rows // DN_CHUNK
    n_tiles = t_len // rows

    def tok(width):
        return pl.BlockSpec((1, rows, width), lambda b, t: (b, t, 0))

    o, s = pl.pallas_call(
        functools.partial(_gdn_prompt_kernel, chunks=chunks),
        grid=(n_seq, n_tiles),
        in_specs=[tok(W_BR), tok(W_BR), tok(W_BR), tok(N_HEAD), tok(N_HEAD),
                  pl.BlockSpec((chunks, N_HEAD, DN_CHUNK), lambda b, t: (b * n_tiles + t, 0, 0))],
        out_specs=[tok(W_BR), pl.BlockSpec((1, N_HEAD, D_HEAD, D_HEAD), lambda b, t: (b, 0, 0, 0))],
        out_shape=[jax.ShapeDtypeStruct((n_seq, t_len, W_BR), F32),
                   jax.ShapeDtypeStruct((n_seq, N_HEAD, D_HEAD, D_HEAD), F32)],
        scratch_shapes=[pltpu.VMEM((N_HEAD, D_HEAD, D_HEAD), F32)],
        compiler_params=_cparams(2),
    )(qc.reshape(n_seq, t_len, W_BR), kc.reshape(n_seq, t_len, W_BR), vc.reshape(n_seq, t_len, W_BR),
      beta.reshape(n_seq, t_len, N_HEAD), gcol.reshape(n_seq, t_len, N_HEAD), grow)
    return o.reshape(n_tok, W_BR), s


def _gdn_sample_kernel(q_ref, k_ref, v_ref, bcol_ref, gcol_ref, grow_ref, s0_ref, o_ref, s_ref,
                       *, t_len):
    n_row = N_HEAD * t_len
    ri = lax.broadcasted_iota(jnp.int32, (n_row, n_row), 0)
    ci = lax.broadcasted_iota(jnp.int32, (n_row, n_row), 1)
    same = (ri // t_len) == (ci // t_len)
    tril = same & (ri >= ci)
    strict = same & (ri > ci)
    q = _stack_heads(q_ref[0]) * (D_HEAD ** -0.5)
    k = _stack_heads(k_ref[0])
    v = _stack_heads(v_ref[0])
    b = bcol_ref[0]
    g_lane = jnp.broadcast_to(gcol_ref[0], (n_row, 128))
    gc = jnp.dot(tril.astype(F32), g_lane, precision=HIGHEST, preferred_element_type=F32)[:, 0:1]
    gl = jnp.dot(same.astype(F32), g_lane, precision=HIGHEST, preferred_element_type=F32)[:, 0:1]
    g_sub = jnp.broadcast_to(grow_ref[0], (SUBLANE, n_row))
    gr = lax.dot_general(g_sub, tril.astype(F32), _NT, precision=HIGHEST,
                         preferred_element_type=F32)[0:1, :]
    decay = jnp.where(tril, jnp.exp(jnp.where(tril, gc - gr, 0.0)), 0.0)
    eg = jnp.exp(gc)
    kb = k * b
    a = jnp.where(strict, lax.dot_general(kb, k, _NT, preferred_element_type=F32) * decay, 0.0)
    intra = lax.dot_general(q, k, _NT, preferred_element_type=F32) * decay
    vb = v * b
    vb_c = vb[:, 0:D_HEAD]
    for h in range(1, N_HEAD):
        vb_c = vb_c + vb[:, h * D_HEAD:(h + 1) * D_HEAD]
    x = _mm3(_unit_lower_inverse(a, t_len), jnp.concatenate([kb * eg, vb_c], axis=1))
    w, u = x[:, :W_BR], x[:, W_BR:]
    s0 = s0_ref[0]
    v_new = u - jnp.dot(w, s0, preferred_element_type=F32)
    o = jnp.dot(q * eg, s0, preferred_element_type=F32) + jnp.dot(intra, v_new, preferred_element_type=F32)
    er = lax.broadcasted_iota(jnp.int32, (W_BR, n_row), 0)
    ec = lax.broadcasted_iota(jnp.int32, (W_BR, n_row), 1)
    pick = ((er // D_HEAD) == (ec // t_len)) & ((ec % t_len) == t_len - 1)
    gl_state = jnp.dot(pick.astype(F32), jnp.broadcast_to(gc, (n_row, 128)), precision=HIGHEST,
                       preferred_element_type=F32)[:, 0:1]
    s_ref[0] = s0 * jnp.exp(gl_state) + lax.dot_general(
        k * jnp.exp(gl - gc), v_new, _TN, preferred_element_type=F32)
    o_ref[0] = _unstack_heads(jnp.concatenate([o] * N_HEAD, axis=1), t_len)


def _gdn_sample_call(qc, kc, vc, bcol, gcol, grow, s0):
    n_seq, t_len, _ = qc.shape
    n_row = N_HEAD * t_len
    assert t_len >= SOLVE_BASE and t_len & (t_len - 1) == 0
    tok = pl.BlockSpec((1, t_len, W_BR), lambda b: (b, 0, 0))
    colv = pl.BlockSpec((1, n_row, 1), lambda b: (b, 0, 0))
    st = pl.BlockSpec((1, W_BR, D_HEAD), lambda b: (b, 0, 0))
    return pl.pallas_call(
        functools.partial(_gdn_sample_kernel, t_len=t_len),
        grid=(n_seq,),
        in_specs=[tok, tok, tok, colv, colv, pl.BlockSpec((1, 1, n_row), lambda b: (b, 0, 0)), st],
        out_specs=[tok, st],
        out_shape=[jax.ShapeDtypeStruct((n_seq, t_len, W_BR), F32),
                   jax.ShapeDtypeStruct((n_seq, W_BR, D_HEAD), F32)],
        compiler_params=_cparams(1),
    )(qc, kc, vc, bcol, gcol, grow, s0)


def _out_kernel(x_ref, sc_ref, sh_ref, gate_ref, g1_ref, attn_ref, yb_ref, od_ref, og_ref, dng_ref,
                wmg_ref, woa_ref, wob_ref, woc_ref, wo_ref, gsum_ref, o_ref):
    x = x_ref[...]
    h = _mod_norm(x, g1_ref[...], sc_ref[0], sh_ref[0]).astype(BF16)
    mg = _sigmoid(jnp.dot(h, wmg_ref[...], preferred_element_type=F32))
    d = x.shape[1]
    ya = jnp.dot(attn_ref[...].astype(BF16), woa_ref[...], preferred_element_type=F32)
    yb = jnp.dot(yb_ref[...].astype(BF16), wob_ref[...], preferred_element_type=F32)
    od = od_ref[...]
    oc = (od * lax.rsqrt(_group_sumsq(od, gsum_ref[...]) * (1.0 / D_HEAD) + EPS) * dng_ref[...]) \
        * _silu(og_ref[...])
    yc = jnp.dot(oc.astype(BF16), woc_ref[...], preferred_element_type=F32)
    merged = mg[:, :d] * ya + mg[:, d:2 * d] * yb + mg[:, 2 * d:] * yc
    o_ref[...] = x + gate_ref[0] * jnp.dot(merged.astype(BF16), wo_ref[...], preferred_element_type=F32)


def _token_grid(n_tok, n_seq, prompt, d_model):
    if prompt:
        rows = TM
        n_tiles = n_tok // rows
        tiles_per_seq = n_tiles // n_seq
        mod_spec = pl.BlockSpec((1, 1, d_model), lambda t: (t // tiles_per_seq, 0, 0))
    else:
        rows, n_tiles, tiles_per_seq = n_tok, 1, 1
        mod_spec = pl.BlockSpec((1, rows, d_model), lambda t: (0, 0, 0))
    return rows, n_tiles, tiles_per_seq, mod_spec


def _out_call(x, sc, sh, gate, attn, yb, od, og, w, *, prompt, n_seq):
    n_tok, d_model = x.shape
    rows, n_tiles, _, mod_spec = _token_grid(n_tok, n_seq, prompt, d_model)

    def tok(width):
        return pl.BlockSpec((rows, width), lambda t: (t, 0))

    return pl.pallas_call(
        _out_kernel, grid=(n_tiles,),
        in_specs=[tok(d_model), mod_spec, mod_spec, mod_spec, _const_spec((1, d_model)),
                  tok(W_BR), tok(W_BR), tok(W_BR), tok(W_BR), _const_spec((1, W_BR)),
                  _const_spec(w['wmg'].shape), _const_spec(w['woa'].shape), _const_spec(w['wob'].shape),
                  _const_spec(w['woc'].shape), _const_spec(w['wo'].shape), _const_spec((W_BR, W_BR))],
        out_specs=tok(d_model),
        out_shape=jax.ShapeDtypeStruct((n_tok, d_model), F32),
        compiler_params=_cparams(1),
    )(x, sc, sh, gate, w['norm1_g'], attn, yb, od, og, w['dn_norm_g'], w['wmg'], w['woa'], w['wob'],
      w['woc'], w['wo'], w['gsum'])


def _ffn_kernel(x_ref, sc_ref, sh_ref, gate_ref, g2_ref, wg_ref, wu_ref, wd_ref, cw_ref, halo_ref,
                o_ref, tail_ref, xp_ref, *, rows, tiles_per_seq, stride, halo):
    first = (pl.program_id(0) % tiles_per_seq) == 0
    x = x_ref[...]
    h = _mod_norm(x, g2_ref[...], sc_ref[0], sh_ref[0]).astype(BF16)
    pre = jnp.dot(h, wg_ref[...], preferred_element_type=F32)
    hg, tail = _causal_conv(xp_ref, pre, cw_ref, halo_ref, first,
                            width=CONV_F, halo=halo, stride=stride, rows=rows)
    tail_ref[0] = tail
    act = _silu(hg) * jnp.dot(h, wu_ref[...], preferred_element_type=F32)
    o_ref[...] = x + gate_ref[0] * jnp.dot(act.astype(BF16), wd_ref[...], preferred_element_type=F32)


def _ffn_call(x, sc, sh, gate, w, halo_in, *, prompt, n_seq):
    n_tok, d_model = x.shape
    d_ff = w['wgate'].shape[1]
    rows, n_tiles, tiles_per_seq, mod_spec = _token_grid(n_tok, n_seq, prompt, d_model)
    stride = 1 if prompt else n_seq
    halo = SUBLANE if prompt else (CONV_F - 1) * n_seq
    n_grp = n_tiles // tiles_per_seq
    tok = pl.BlockSpec((rows, d_model), lambda t: (t, 0))
    per_seq = pl.BlockSpec((1, halo, d_ff), lambda t: (t // tiles_per_seq, 0, 0))
    kern = functools.partial(_ffn_kernel, rows=rows, tiles_per_seq=tiles_per_seq, stride=stride, halo=halo)
    return pl.pallas_call(
        kern, grid=(n_tiles,),
        in_specs=[tok, mod_spec, mod_spec, mod_spec, _const_spec((1, d_model)),
                  _const_spec(w['wgate'].shape), _const_spec(w['wup'].shape), _const_spec(w['wdown'].shape),
                  _const_spec((CONV_F, d_ff)), per_seq],
        out_specs=[tok, per_seq],
        out_shape=[jax.ShapeDtypeStruct((n_tok, d_model), F32),
                   jax.ShapeDtypeStruct((n_grp, halo, d_ff), F32)],
        scratch_shapes=[pltpu.VMEM((halo + rows, d_ff), F32)],
        compiler_params=_cparams(1),
    )(x, sc, sh, gate, w['norm2_g'], w['wgate'], w['wup'], w['wdown'], w['ffn_conv_w'], halo_in)


def _t5_bucket(rel):
    n = jnp.maximum(rel, 0)
    max_exact = REL_BUCKETS // 2
    nf = jnp.maximum(n, 1).astype(F32)
    large = max_exact + (jnp.log(nf / max_exact) / math.log(REL_MAX_DIST / max_exact)
                         * (REL_BUCKETS - max_exact)).astype(jnp.int32)
    large = jnp.minimum(large, REL_BUCKETS - 1)
    return jnp.where(n < max_exact, n, large)


def _bias_tables(rel_bias, n_blk_prompt, past_len, t_s):
    blk = MOBA_BLOCK

    def lookup(rel):
        onehot = jax.nn.one_hot(_t5_bucket(rel).reshape(-1), REL_BUCKETS, dtype=F32)
        tab = lax.dot_general(rel_bias, onehot, (((0,), (1,)), ((), ())), precision=HIGHEST)
        return jnp.where(rel.reshape(-1) >= 0, tab, NEG).reshape((N_HEAD,) + rel.shape)

    n_dist = min(n_blk_prompt, -(-(REL_MAX_DIST + blk - 1) // blk) + 1)
    d = jnp.arange(n_dist, dtype=jnp.int32)[:, None, None]
    ki = jnp.arange(blk, dtype=jnp.int32)[None, :, None]
    qj = jnp.arange(blk, dtype=jnp.int32)[None, None, :]
    tab_p = lookup(d * blk + qj - ki)
    n_blk_s = past_len // blk
    nb = jnp.arange(n_blk_s, dtype=jnp.int32)[:, None, None]
    tq = jnp.arange(t_s, dtype=jnp.int32)[None, :, None]
    ko = jnp.arange(blk, dtype=jnp.int32)[None, None, :]
    tab_s = lookup(past_len + tq - nb * blk - ko)
    tab_s = jnp.transpose(tab_s, (1, 0, 2, 3)).reshape(n_blk_s, N_HEAD * t_s, blk)
    rel_o = jnp.arange(t_s, dtype=jnp.int32)[:, None] - jnp.arange(128, dtype=jnp.int32)[None, :]
    tab_o = lookup(rel_o).reshape(N_HEAD * t_s, 128)
    return tab_p.astype(F32), tab_s.astype(F32), tab_o.astype(F32)


def _layer_weights(l, p, gsum):
    w_in = p['w_in'][l]
    cuts = [0, 3 * W_BR, 6 * W_BR, 10 * W_BR, 10 * W_BR + 2 * N_HEAD]
    wba = w_in[:, cuts[3]:cuts[4]].astype(BF16)
    row = lambda a: a.reshape(1, -1)
    return {
        'wa': w_in[:, cuts[0]:cuts[1]].astype(BF16),
        'wb': w_in[:, cuts[1]:cuts[2]].astype(BF16),
        'wc': w_in[:, cuts[2]:cuts[3]].astype(BF16),
        'wba': wba, 'wbat': wba.T,
        'wmg': w_in[:, cuts[4]:].astype(BF16),
        'norm1_g': row(p['norm1_g'][l]), 'norm2_g': row(p['norm2_g'][l]),
        'qn_g': row(jnp.tile(p['qn_g'][l], N_HEAD)), 'kn_g': row(jnp.tile(p['kn_g'][l], N_HEAD)),
        'dn_norm_g': row(jnp.tile(p['dn_norm_g'][l], N_HEAD)),
        'conv_b_w': p['conv_b_w'][l], 'dn_conv_w': p['dn_conv_w'][l], 'ffn_conv_w': p['ffn_conv_w'][l],
        'a_log_row': row(p['dn_a_log'][l]), 'dt_bias_row': row(p['dn_dt_bias'][l]),
        'a_log_col': p['dn_a_log'][l].reshape(-1, 1), 'dt_bias_col': p['dn_dt_bias'][l].reshape(-1, 1),
        'woa': p['w_oa'][l].astype(BF16), 'wob': p['w_ob'][l].astype(BF16), 'woc': p['w_oc'][l].astype(BF16),
        'wo': p['w_o'][l].astype(BF16),
        'wgate': p['w_gate'][l].astype(BF16), 'wup': p['w_up'][l].astype(BF16),
        'wdown': p['w_down'][l].astype(BF16),
        'gsum': gsum,
    }


def kernel(x_prompt, x_sample, cache_k, cache_v, state_conv_b, state_conv_dn, state_dn, state_conv_ffn,
           page_table, c_prompt, c_sample, rel_bias, w_ada, b_ada, norm1_g, norm2_g, w_in, qn_g, kn_g,
           conv_b_w, dn_conv_w, dn_a_log, dn_dt_bias, dn_norm_g, w_oa, w_ob, w_oc, w_o, w_gate, w_up,
           ffn_conv_w, w_down):
    bp, tp, d_model = x_prompt.shape
    bs, ts, _ = x_sample.shape
    depth = w_in.shape[0]
    d_ff = w_gate.shape[2]
    n_pool, page = cache_k.shape[1], cache_k.shape[2]
    past_len = page_table.shape[1] * page
    assert tp % MOBA_BLOCK == 0 and past_len % MOBA_BLOCK == 0 and MOBA_BLOCK == 2 * page
    assert ts <= 128 and past_len // MOBA_BLOCK >= MOBA_TOPK and (bs * ts) % SUBLANE == 0
    p = dict(w_in=w_in, norm1_g=norm1_g, norm2_g=norm2_g, qn_g=qn_g, kn_g=kn_g, conv_b_w=conv_b_w,
             dn_conv_w=dn_conv_w, dn_a_log=dn_a_log, dn_dt_bias=dn_dt_bias, dn_norm_g=dn_norm_g,
             w_oa=w_oa, w_ob=w_ob, w_oc=w_oc, w_o=w_o, w_gate=w_gate, w_up=w_up,
             ffn_conv_w=ffn_conv_w, w_down=w_down)

    lane = jnp.arange(W_BR) // D_HEAD
    gsum = (lane[:, None] == lane[None, :]).astype(BF16)
    tab_p, tab_s, tab_o = _bias_tables(rel_bias, tp // MOBA_BLOCK, past_len, ts)

    n_c = bp + bs
    c_rows = -(-n_c // SUBLANE) * SUBLANE
    c_all = jnp.concatenate([c_prompt, c_sample, jnp.zeros((c_rows - n_c, d_model), F32)], axis=0)
    mods = _ada_call(c_all, w_ada, b_ada)

    ck = jnp.transpose(cache_k, (0, 1, 3, 4, 2))
    cv = jnp.transpose(cache_v, (0, 1, 3, 4, 2))

    xp = x_prompt.reshape(bp * tp, d_model)
    xs = jnp.transpose(x_sample, (1, 0, 2)).reshape(ts * bs, d_model)

    def to_tm(a):
        return jnp.transpose(a, (1, 0, 2)).reshape(1, a.shape[1] * bs, a.shape[2])

    def from_tm(a, r):
        return jnp.transpose(a.reshape(r, bs, a.shape[-1]), (1, 0, 2))

    zero_b = jnp.zeros((bp, SUBLANE, W_BR), F32)
    zero_c = jnp.zeros((bp, SUBLANE, 3 * W_BR), F32)
    zero_f = jnp.zeros((bp, SUBLANE, d_ff), F32)

    outs = [[] for _ in range(12)]
    for l in range(depth):
        w = _layer_weights(l, p, gsum)
        mod_p = [m.reshape(bp, 1, d_model) for m in jnp.split(mods[l, :bp], 6, axis=-1)]
        mod_s = [jnp.tile(m, (ts, 1)).reshape(1, ts * bs, d_model)
                 for m in jnp.split(mods[l, bp:bp + bs], 6, axis=-1)]

        sh1, sc1, g1, sh2, sc2, g2 = mod_p
        (q, k, v, yb, tail_b, qc, kc, vc, og, beta, gcol, tail_c, k_bf, vt_bf, kmean, grow) = _in_call(
            xp, sc1, sh1, w, prompt=True, n_seq=bp, halo_b_in=zero_b, halo_c_in=zero_c)
        attn = _attn_prompt_call(q, k_bf, vt_bf, kmean, tab_p, bp)
        od, s_p = _gdn_prompt_call(qc, kc, vc, beta, gcol, grow, bp)
        xp = _out_call(xp, sc1, sh1, g1, attn, yb, od, og, w, prompt=True, n_seq=bp)
        xp, tail_f = _ffn_call(xp, sc2, sh2, g2, w, zero_f, prompt=True, n_seq=bp)
        outs[0].append(k.reshape(bp, tp, N_HEAD, D_HEAD))
        outs[1].append(v.reshape(bp, tp, N_HEAD, D_HEAD))
        outs[4].append(tail_b[:, SUBLANE - (CONV_B - 1):])
        outs[6].append(tail_c[:, SUBLANE - (CONV_C - 1):])
        outs[8].append(s_p)
        outs[10].append(tail_f[:, SUBLANE - (CONV_F - 1):])

        sh1, sc1, g1, sh2, sc2, g2 = mod_s
        (q, k, v, yb, tail_b, qc, kc, vc, og, beta, gcol, tail_c) = _in_call(
            xs, sc1, sh1, w, prompt=False, n_seq=bs,
            halo_b_in=to_tm(state_conv_b[l]), halo_c_in=to_tm(state_conv_dn[l]))
        q_b, k_b, v_b = from_tm(q, ts), from_tm(k, ts), from_tm(v, ts)
        attn = _attn_sample_call(page_table, q_b, k_b, v_b, ck, cv, l, tab_s, tab_o)
        beta_ht = jnp.transpose(beta.reshape(ts, bs, N_HEAD), (1, 2, 0)).reshape(bs, N_HEAD * ts)
        g_ht = jnp.transpose(gcol.reshape(ts, bs, N_HEAD), (1, 2, 0)).reshape(bs, N_HEAD * ts)
        od, s_s = _gdn_sample_call(from_tm(qc, ts), from_tm(kc, ts), from_tm(vc, ts),
                                   beta_ht[:, :, None], g_ht[:, :, None], g_ht[:, None, :],
                                   state_dn[l].reshape(bs, W_BR, D_HEAD))
        attn_tm = jnp.transpose(attn, (1, 0, 2)).reshape(ts * bs, W_BR)
        od_tm = jnp.transpose(od, (1, 0, 2)).reshape(ts * bs, W_BR)
        xs = _out_call(xs, sc1, sh1, g1, attn_tm, yb, od_tm, og, w, prompt=False, n_seq=bs)
        xs, tail_f = _ffn_call(xs, sc2, sh2, g2, w, to_tm(state_conv_ffn[l]), prompt=False, n_seq=bs)
        outs[2].append(k_b.reshape(bs, ts, N_HEAD, D_HEAD))
        outs[3].append(v_b.reshape(bs, ts, N_HEAD, D_HEAD))
        outs[5].append(from_tm(tail_b[0], CONV_B - 1))
        outs[7].append(from_tm(tail_c[0], CONV_C - 1))
        outs[9].append(s_s.reshape(bs, N_HEAD, D_HEAD, D_HEAD))
        outs[11].append(from_tm(tail_f[0], CONV_F - 1))

    y_p = xp.reshape(bp, tp, d_model)
    y_s = from_tm(xs, ts)
    return (y_p, y_s) + tuple(jnp.stack(o) for o in outs)
```

```python
import functools
import math

import jax
import jax.numpy as jnp
from jax import lax
from jax.experimental import pallas as pl
from jax.experimental.pallas import tpu as pltpu

F32 = jnp.float32
BF16 = jnp.bfloat16

N_HEAD = 8
D_HEAD = 64
W_BR = N_HEAD * D_HEAD
MOBA_BLOCK = 256
MOBA_TOPK = 3
CONV_B = 3
CONV_C = 4
CONV_F = 3
DN_CHUNK = 64
REL_BUCKETS = 32
REL_MAX_DIST = 2048
N_BRANCH = 3
EPS = 1e-6
NEG = -1e30

TM = MOBA_BLOCK
SUBLANE = 8
VMEM_LIMIT = 56 * 1024 * 1024
HIGHEST = lax.Precision.HIGHEST

_NT = (((1,), (1,)), ((), ()))
_TN = (((0,), (0,)), ((), ()))


def _const_spec(shape):
    nd = len(shape)
    return pl.BlockSpec(shape, lambda *_: (0,) * nd, pipeline_mode=pl.Buffered(1))


def _cparams(n_grid):
    return pltpu.CompilerParams(dimension_semantics=("arbitrary",) * n_grid,
                                vmem_limit_bytes=VMEM_LIMIT)


def _sigmoid(x):
    return 1.0 / (1.0 + jnp.exp(-x))


def _silu(x):
    return x * _sigmoid(x)


def _softplus(x):
    return jnp.maximum(x, 0.0) + jnp.log1p(jnp.exp(-jnp.abs(x)))


def _split_dot(a, b_bf16):
    hi = a.astype(BF16)
    lo = (a - hi.astype(F32)).astype(BF16)
    return (jnp.dot(hi, b_bf16, preferred_element_type=F32)
            + jnp.dot(lo, b_bf16, preferred_element_type=F32))


def _dot(a, b):
    if a.ndim == 3:
        return lax.dot_general(a, b, (((2,), (1,)), ((0,), (0,))), preferred_element_type=F32)
    return jnp.dot(a, b, preferred_element_type=F32)


def _mm3(a, b):
    ah = a.astype(BF16)
    al = (a - ah.astype(F32)).astype(BF16)
    bh = b.astype(BF16)
    bl = (b - bh.astype(F32)).astype(BF16)
    return _dot(ah, bh) + _dot(ah, bl) + _dot(al, bh)


def _group_sumsq(y, gsum_bf16):
    return _split_dot(y * y, gsum_bf16)


def _mod_norm(x, gain, sc, sh):
    ms = jnp.mean(x * x, axis=-1, keepdims=True)
    return (x * lax.rsqrt(ms + EPS) * gain) * (1.0 + sc) + sh


def _causal_conv(xp_ref, cur, w_ref, halo_ref, first, *, width, halo, stride, rows):
    @pl.when(first)
    def _():
        xp_ref[0:halo, :] = halo_ref[0]

    xp_ref[halo:halo + rows, :] = cur
    off = halo - (width - 1) * stride
    y = xp_ref[off:off + rows, :] * w_ref[0:1, :]
    for i in range(1, width):
        off = halo - (width - 1 - i) * stride
        y = y + xp_ref[off:off + rows, :] * w_ref[i:i + 1, :]
    tail = xp_ref[rows:rows + halo, :]
    xp_ref[0:halo, :] = tail
    return y, tail


def _ada_kernel(c_ref, w_ref, b_ref, o_ref):
    c = _silu(c_ref[...]).astype(BF16)
    o_ref[0] = jnp.dot(c, w_ref[0].astype(BF16), preferred_element_type=F32) + b_ref[0]


def _ada_call(c_all, w_ada, b_ada):
    depth, d_model, n_mod = w_ada.shape
    rows = c_all.shape[0]
    bn = 1536
    return pl.pallas_call(
        _ada_kernel,
        grid=(depth, n_mod // bn),
        in_specs=[pl.BlockSpec((rows, d_model), lambda l, j: (0, 0)),
                  pl.BlockSpec((1, d_model, bn), lambda l, j: (l, 0, j)),
                  pl.BlockSpec((1, 1, bn), lambda l, j: (l, 0, j))],
        out_specs=pl.BlockSpec((1, rows, bn), lambda l, j: (l, 0, j)),
        out_shape=jax.ShapeDtypeStruct((depth, rows, n_mod), F32),
        compiler_params=_cparams(2),
    )(c_all, w_ada, b_ada.reshape(depth, 1, n_mod))


def _in_kernel(*refs, prompt, rows, tiles_per_seq, stride, halo_b, halo_c):
    (x_ref, sc_ref, sh_ref, g1_ref, wa_ref, wb_ref, wc_ref, wba_ref, wbat_ref, qn_ref, kn_ref,
     cbw_ref, cdw_ref, alog_ref, dtb_ref, alogc_ref, dtbc_ref, gsum_ref, hb_ref, hc_ref) = refs[:20]
    outs = refs[20:-2]
    xpb_ref, xpc_ref = refs[-2:]
    if prompt:
        (q_ref, k_ref, v_ref, yb_ref, tailb_ref, qc_ref, kc_ref, vc_ref, og_ref, beta_ref,
         gcol_ref, tailc_ref, kbf_ref, vt_ref, kmean_ref, grow_ref) = outs
    else:
        (q_ref, k_ref, v_ref, yb_ref, tailb_ref, qc_ref, kc_ref, vc_ref, og_ref, beta_ref,
         gcol_ref, tailc_ref) = outs

    first = (pl.program_id(0) % tiles_per_seq) == 0
    gsum = gsum_ref[...]
    h = _mod_norm(x_ref[...], g1_ref[...], sc_ref[0], sh_ref[0])
    hb16 = h.astype(BF16)

    za = jnp.dot(hb16, wa_ref[...], preferred_element_type=F32)
    qa, ka, va = za[:, :W_BR], za[:, W_BR:2 * W_BR], za[:, 2 * W_BR:]
    q = (qa * lax.rsqrt(_group_sumsq(qa, gsum) * (1.0 / D_HEAD) + EPS) * qn_ref[...]) * (D_HEAD ** -0.5)
    k = ka * lax.rsqrt(_group_sumsq(ka, gsum) * (1.0 / D_HEAD) + EPS) * kn_ref[...]
    q_ref[...] = q
    k_ref[...] = k
    v_ref[...] = va
    if prompt:
        kbf_ref[0] = k.astype(BF16)
        vt_ref[0] = va.T.astype(BF16)
        kmean_ref[0] = jnp.sum(k, axis=0, keepdims=True) * (1.0 / MOBA_BLOCK)

    zb = jnp.dot(hb16, wb_ref[...], preferred_element_type=F32)
    hb, bg, cg = zb[:, :W_BR], zb[:, W_BR:2 * W_BR], zb[:, 2 * W_BR:]
    uc, tail_b = _causal_conv(xpb_ref, cg * hb, cbw_ref, hb_ref, first,
                              width=CONV_B, halo=halo_b, stride=stride, rows=rows)
    yb_ref[...] = bg * uc
    tailb_ref[0] = tail_b

    zc = jnp.dot(hb16, wc_ref[...], preferred_element_type=F32)
    qkv, tail_c = _causal_conv(xpc_ref, zc[:, :3 * W_BR], cdw_ref, hc_ref, first,
                               width=CONV_C, halo=halo_c, stride=stride, rows=rows)
    tailc_ref[0] = tail_c
    og_ref[...] = zc[:, 3 * W_BR:]
    qkv = _silu(qkv)
    qc, kc, vc = qkv[:, :W_BR], qkv[:, W_BR:2 * W_BR], qkv[:, 2 * W_BR:]
    qc_ref[...] = qc * lax.rsqrt(_group_sumsq(qc, gsum) + EPS)
    kc_ref[...] = kc * lax.rsqrt(_group_sumsq(kc, gsum) + EPS)
    vc_ref[...] = vc
    zba = jnp.dot(hb16, wba_ref[...], preferred_element_type=F32)
    beta_ref[...] = _sigmoid(zba[:, :N_HEAD])
    gcol_ref[...] = -jnp.exp(alog_ref[...]) * _softplus(zba[:, N_HEAD:] + dtb_ref[...])
    if prompt:
        zbat = lax.dot_general(wbat_ref[...], hb16, _NT, preferred_element_type=F32)
        g_row = -jnp.exp(alogc_ref[...]) * _softplus(zbat[N_HEAD:, :] + dtbc_ref[...])
        for j in range(rows // DN_CHUNK):
            grow_ref[j] = g_row[:, j * DN_CHUNK:(j + 1) * DN_CHUNK]


def _in_call(x, sc, sh, w, *, prompt, n_seq, halo_b_in, halo_c_in):
    n_tok, d_model = x.shape
    if prompt:
        rows, stride = TM, 1
        n_tiles = n_tok // rows
        tiles_per_seq = n_tiles // n_seq
        halo_b = halo_c = SUBLANE
        mod_spec = pl.BlockSpec((1, 1, d_model), lambda t: (t // tiles_per_seq, 0, 0))
    else:
        rows, stride = n_tok, n_seq
        n_tiles, tiles_per_seq = 1, 1
        halo_b, halo_c = (CONV_B - 1) * n_seq, (CONV_C - 1) * n_seq
        mod_spec = pl.BlockSpec((1, rows, d_model), lambda t: (0, 0, 0))
    n_grp = n_tiles // tiles_per_seq

    def tok(width):
        return pl.BlockSpec((rows, width), lambda t: (t, 0))

    def per_seq(r, width):
        return pl.BlockSpec((1, r, width), lambda t: (t // tiles_per_seq, 0, 0))

    in_specs = [tok(d_model), mod_spec, mod_spec, _const_spec((1, d_model)),
                _const_spec(w['wa'].shape), _const_spec(w['wb'].shape), _const_spec(w['wc'].shape),
                _const_spec(w['wba'].shape), _const_spec(w['wbat'].shape),
                _const_spec((1, W_BR)), _const_spec((1, W_BR)),
                _const_spec((CONV_B, W_BR)), _const_spec((CONV_C, 3 * W_BR)),
                _const_spec((1, N_HEAD)), _const_spec((1, N_HEAD)),
                _const_spec((N_HEAD, 1)), _const_spec((N_HEAD, 1)),
                _const_spec((W_BR, W_BR)),
                per_seq(halo_b, W_BR), per_seq(halo_c, 3 * W_BR)]
    out_specs = [tok(W_BR), tok(W_BR), tok(W_BR), tok(W_BR), per_seq(halo_b, W_BR),
                 tok(W_BR), tok(W_BR), tok(W_BR), tok(W_BR), tok(N_HEAD), tok(N_HEAD),
                 per_seq(halo_c, 3 * W_BR)]
    out_shape = [jax.ShapeDtypeStruct((n_tok, W_BR), F32)] * 4 + [
        jax.ShapeDtypeStruct((n_grp, halo_b, W_BR), F32)] + [
        jax.ShapeDtypeStruct((n_tok, W_BR), F32)] * 4 + [
        jax.ShapeDtypeStruct((n_tok, N_HEAD), F32)] * 2 + [
        jax.ShapeDtypeStruct((n_grp, halo_c, 3 * W_BR), F32)]
    if prompt:
        cpt = rows // DN_CHUNK
        out_specs += [pl.BlockSpec((1, rows, W_BR), lambda t: (t, 0, 0)),
                      pl.BlockSpec((1, W_BR, rows), lambda t: (t, 0, 0)),
                      pl.BlockSpec((1, 1, W_BR), lambda t: (t, 0, 0)),
                      pl.BlockSpec((cpt, N_HEAD, DN_CHUNK), lambda t: (t, 0, 0))]
        out_shape += [jax.ShapeDtypeStruct((n_tiles, rows, W_BR), BF16),
                      jax.ShapeDtypeStruct((n_tiles, W_BR, rows), BF16),
                      jax.ShapeDtypeStruct((n_tiles, 1, W_BR), F32),
                      jax.ShapeDtypeStruct((n_tiles * cpt, N_HEAD, DN_CHUNK), F32)]
    kern = functools.partial(_in_kernel, prompt=prompt, rows=rows, tiles_per_seq=tiles_per_seq,
                             stride=stride, halo_b=halo_b, halo_c=halo_c)
    return pl.pallas_call(
        kern, grid=(n_tiles,), in_specs=in_specs, out_specs=out_specs, out_shape=out_shape,
        scratch_shapes=[pltpu.VMEM((halo_b + rows, W_BR), F32),
                        pltpu.VMEM((halo_c + rows, 3 * W_BR), F32)],
        compiler_params=_cparams(1),
    )(x, sc, sh, w['norm1_g'], w['wa'], w['wb'], w['wc'], w['wba'], w['wbat'], w['qn_g'], w['kn_g'],
      w['conv_b_w'], w['dn_conv_w'], w['a_log_row'], w['dt_bias_row'], w['a_log_col'],
      w['dt_bias_col'], w['gsum'], halo_b_in, halo_c_in)


def _gate_scores(a, b):
    return lax.dot_general(a.astype(BF16), b.astype(BF16), _NT, preferred_element_type=F32)


def _top3_mask_cols(gates, own, n_blk):
    blk = lax.broadcasted_iota(jnp.int32, gates.shape, 0)
    g = jnp.where(blk < own, gates, NEG)
    mask = jnp.full(gates.shape, NEG, F32)
    for _ in range(MOBA_TOPK):
        mx = jnp.max(g, axis=0, keepdims=True)
        idx = jnp.min(jnp.where(g == mx, blk, n_blk), axis=0, keepdims=True)
        hit = blk == idx
        mask = jnp.where(hit & (idx < own), 0.0, mask)
        g = jnp.where(hit, -jnp.inf, g)
    return mask


def _attn_prompt_kernel(q_ref, k_ref, vt_ref, kmean_ref, bias_ref, o_ref, sel_ref, s_ref, p_ref,
                        *, n_blk, n_dist):
    qt = pl.program_id(2)
    qf = q_ref[0]
    lane = lax.broadcasted_iota(jnp.int32, qf.shape, 1)
    ones = jnp.ones((16, MOBA_BLOCK), BF16)
    qbs = []
    for hh in range(2):
        qh = jnp.where(lane // D_HEAD == hh, qf, 0.0)
        sel_ref[hh] = _top3_mask_cols(_gate_scores(kmean_ref[0], qh), qt, n_blk)
        qbs.append(qh.astype(BF16))

    def lhs_v(hh, n):
        return jnp.concatenate([vt_ref[0, n, hh * D_HEAD:(hh + 1) * D_HEAD, :], ones], axis=0)

    def scores(hh, n):
        return lax.dot_general(k_ref[0, n], qbs[hh], _NT, preferred_element_type=F32)

    def masked_scores(hh, n):
        d = jnp.minimum(qt - n, n_dist - 1)
        return scores(hh, n) + bias_ref[hh, d] + sel_ref[hh, pl.ds(n, 1), :]

    carry = []
    for hh in range(2):
        s = scores(hh, qt) + bias_ref[hh, 0]
        m = jnp.max(s, axis=0, keepdims=True)
        p_ref[1, hh] = jnp.exp(s - m).astype(BF16)
        carry.append((m, jnp.ones_like(m), jnp.zeros((D_HEAD + 16, TM), F32)))
        s_ref[0, hh] = masked_scores(hh, 0)
    carry = tuple(carry)

    def pending_pv(hh, n, slot):
        v_blk = jnp.where(n == 0, qt, n - 1)
        return jnp.dot(lhs_v(hh, v_blk), p_ref[1 - slot, hh], preferred_element_type=F32)

    def step(n, carry):
        slot = n % 2
        nxt = [masked_scores(hh, n + 1) for hh in range(2)]
        pv = [pending_pv(hh, n, slot) for hh in range(2)]
        out = []
        for hh in range(2):
            m, alpha, acc = carry[hh]
            s = s_ref[slot, hh]
            m_new = jnp.maximum(m, jnp.max(s, axis=0, keepdims=True))
            p_ref[slot, hh] = jnp.exp(s - m_new).astype(BF16)
            out.append((m_new, jnp.exp(m - m_new), alpha * acc + pv[hh]))
        for hh in range(2):
            s_ref[1 - slot, hh] = nxt[hh]
        return tuple(out)

    carry = lax.fori_loop(0, qt, step, carry)
    outs = []
    for hh in range(2):
        _, alpha, acc = carry[hh]
        acc = alpha * acc + pending_pv(hh, qt, qt % 2)
        outs.append(acc[:D_HEAD] / acc[D_HEAD:D_HEAD + 1])
    o_ref[0] = jnp.concatenate(outs, axis=0).T


def _attn_prompt_call(q, k_bf, vt_bf, kmean, bias_tab, n_seq):
    n_tok = q.shape[0]
    t_len = n_tok // n_seq
    n_blk = t_len // MOBA_BLOCK
    n_dist = bias_tab.shape[1]
    q3 = q.reshape(n_seq, t_len, W_BR)
    k4 = k_bf.reshape(n_seq, n_blk, MOBA_BLOCK, W_BR)
    vt4 = vt_bf.reshape(n_seq, n_blk, W_BR, MOBA_BLOCK)
    km3 = kmean.reshape(n_seq, n_blk, W_BR)
    kern = functools.partial(_attn_prompt_kernel, n_blk=n_blk, n_dist=n_dist)
    out = pl.pallas_call(
        kern,
        grid=(n_seq, N_HEAD // 2, n_blk),
        in_specs=[pl.BlockSpec((1, TM, 128), lambda b, hp, t: (b, t, hp)),
                  pl.BlockSpec((1, n_blk, MOBA_BLOCK, 128), lambda b, hp, t: (b, 0, 0, hp)),
                  pl.BlockSpec((1, n_blk, 128, MOBA_BLOCK), lambda b, hp, t: (b, 0, hp, 0)),
                  pl.BlockSpec((1, n_blk, 128), lambda b, hp, t: (b, 0, hp)),
                  pl.BlockSpec((2, n_dist, MOBA_BLOCK, MOBA_BLOCK), lambda b, hp, t: (hp, 0, 0, 0))],
        out_specs=pl.BlockSpec((1, TM, 128), lambda b, hp, t: (b, t, hp)),
        out_shape=jax.ShapeDtypeStruct((n_seq, t_len, W_BR), F32),
        scratch_shapes=[pltpu.VMEM((2, n_blk, TM), F32),
                        pltpu.VMEM((2, 2, MOBA_BLOCK, TM), F32),
                        pltpu.VMEM((2, 2, MOBA_BLOCK, TM), BF16)],
        compiler_params=_cparams(3),
    )(q3, k4, vt4, km3, bias_tab)
    return out.reshape(n_tok, W_BR)


def _stack_heads(x):
    t_len = x.shape[0]
    tiled = jnp.concatenate([x] * N_HEAD, axis=0)
    row = lax.broadcasted_iota(jnp.int32, tiled.shape, 0)
    lane = lax.broadcasted_iota(jnp.int32, tiled.shape, 1)
    return jnp.where(row // t_len == lane // D_HEAD, tiled, 0.0)


def _unstack_heads(y, t_len):
    row = lax.broadcasted_iota(jnp.int32, y.shape, 0)
    lane = lax.broadcasted_iota(jnp.int32, y.shape, 1)
    y = jnp.where(row // t_len == lane // D_HEAD, y, 0.0)
    out = y[0:t_len]
    for h in range(1, N_HEAD):
        out = out + y[h * t_len:(h + 1) * t_len]
    return out


_BQK = (((2,), (2,)), ((0,), (0,)))
_BPV = (((2,), (1,)), ((0,), (0,)))


def _split_heads(x):
    return jnp.stack([x[:, h * D_HEAD:(h + 1) * D_HEAD] for h in range(N_HEAD)], axis=0)


SAMPLE_BLOCKS_PER_STEP = 2
PAGES_PER_BLOCK = 2


def _attn_sample_kernel(pt_ref, q_ref, kn_ref, vn_ref, *refs, n_blk, t_len, page):
    del pt_ref
    n_pg = SAMPLE_BLOCKS_PER_STEP * PAGES_PER_BLOCK
    ck_refs, cv_refs = refs[:n_pg], refs[n_pg:2 * n_pg]
    bias_ref, bown_ref, o_ref, qbd_ref, kmean_ref, m_ref, l_ref, oacc_ref = refs[2 * n_pg:]
    step = pl.program_id(1)
    n_row = N_HEAD * t_len

    @pl.when(step == 0)
    def _():
        qbd_ref[...] = _stack_heads(q_ref[0]).astype(BF16)
        m_ref[...] = jnp.zeros(m_ref.shape, F32)
        l_ref[...] = jnp.zeros(l_ref.shape, F32)
        kmean_ref[...] = jnp.zeros(kmean_ref.shape, F32)

    def own_heads(wide):
        return jnp.concatenate([wide[h * t_len:(h + 1) * t_len, h * D_HEAD:(h + 1) * D_HEAD]
                                for h in range(N_HEAD)], axis=0)

    qbd = qbd_ref[...]
    col = lax.broadcasted_iota(jnp.int32, m_ref.shape, 1)
    kcol = lax.broadcasted_iota(jnp.int32, kmean_ref.shape, 2)
    m_all, l_all, kmean_all = m_ref[...], l_ref[...], kmean_ref[...]
    for j in range(SAMPLE_BLOCKS_PER_STEP):
        n = step * SAMPLE_BLOCKS_PER_STEP + j
        s_pg, ksum = [], None
        for i in range(PAGES_PER_BLOCK):
            kt = ck_refs[PAGES_PER_BLOCK * j + i][0, 0]
            part = jnp.sum(kt, axis=2, keepdims=True)
            ksum = part if ksum is None else ksum + part
            s_pg.append(jnp.dot(qbd, kt.reshape(W_BR, page).astype(BF16), preferred_element_type=F32))
        kmean_all = jnp.where(kcol == n, ksum * (1.0 / MOBA_BLOCK), kmean_all)
        s = jnp.concatenate(s_pg, axis=1) + bias_ref[n]
        m = jnp.max(s, axis=1, keepdims=True)
        p = jnp.exp(s - m)
        l = jnp.sum(p, axis=1, keepdims=True)
        pb = p.astype(BF16)
        pv = None
        for i in range(PAGES_PER_BLOCK):
            vt = cv_refs[PAGES_PER_BLOCK * j + i][0, 0].reshape(W_BR, page).astype(BF16)
            part = lax.dot_general(pb[:, i * page:(i + 1) * page], vt, _NT,
                                   preferred_element_type=F32)
            pv = part if pv is None else pv + part
        oacc_ref[n] = own_heads(pv)
        m_all = jnp.where(col == n, m, m_all)
        l_all = jnp.where(col == n, l, l_all)
    m_ref[...] = m_all
    l_ref[...] = l_all
    kmean_ref[...] = kmean_all

    @pl.when(step == pl.num_programs(1) - 1)
    def _():
        gates = jnp.dot(qbd, kmean_all.reshape(W_BR, 128).astype(BF16),
                        preferred_element_type=F32)[:, 0:n_blk]
        blk = lax.broadcasted_iota(jnp.int32, gates.shape, 1)
        g = gates
        sel = jnp.zeros(gates.shape, jnp.bool_)
        for _ in range(MOBA_TOPK):
            mx = jnp.max(g, axis=1, keepdims=True)
            idx = jnp.min(jnp.where(g == mx, blk, n_blk), axis=1, keepdims=True)
            hit = blk == idx
            sel = sel | hit
            g = jnp.where(hit, -jnp.inf, g)
        m_all = m_ref[:, 0:n_blk]
        l_all = l_ref[:, 0:n_blk]

        pad = jnp.zeros((128 - t_len, W_BR), F32)
        kown = jnp.concatenate([kn_ref[0], pad], axis=0).astype(BF16)
        vown = jnp.concatenate([vn_ref[0], pad], axis=0).astype(BF16)
        s_own = lax.dot_general(qbd, kown, _NT, preferred_element_type=F32) + bown_ref[...]
        m_own = jnp.max(s_own, axis=1, keepdims=True)
        m_tot = jnp.maximum(m_own, jnp.max(jnp.where(sel, m_all, -jnp.inf), axis=1, keepdims=True))
        p_own = jnp.exp(s_own - m_tot)
        wgt = jnp.where(sel, jnp.exp(m_all - m_tot), 0.0)
        denom = jnp.sum(p_own, axis=1, keepdims=True) + jnp.sum(wgt * l_all, axis=1, keepdims=True)
        acc = own_heads(jnp.dot(p_own.astype(BF16), vown, preferred_element_type=F32))
        for i in range(n_blk):
            acc = acc + wgt[:, i:i + 1] * oacc_ref[i]
        out = acc / denom
        o_ref[0] = jnp.concatenate([out[h * t_len:(h + 1) * t_len] for h in range(N_HEAD)], axis=1)


def _attn_sample_call(page_table, q, k_new, v_new, cache_k, cache_v, layer, bias_tab, bias_own):
    n_seq, t_len, _ = q.shape
    n_pages = page_table.shape[1]
    page = cache_k.shape[4]
    n_blk = n_pages * page // MOBA_BLOCK
    n_row = N_HEAD * t_len
    tok = pl.BlockSpec((1, t_len, W_BR), lambda b, n, pt: (b, 0, 0))

    n_pg = SAMPLE_BLOCKS_PER_STEP * PAGES_PER_BLOCK
    assert page * PAGES_PER_BLOCK == MOBA_BLOCK and n_blk % SAMPLE_BLOCKS_PER_STEP == 0 and n_blk <= 128

    def page_spec(j):
        return pl.BlockSpec((1, 1, N_HEAD, D_HEAD, page),
                            lambda b, n, pt: (layer, pt[b, n_pg * n + j], 0, 0, 0))

    pages = [page_spec(j) for j in range(n_pg)]
    kern = functools.partial(_attn_sample_kernel, n_blk=n_blk, t_len=t_len, page=page)
    return pl.pallas_call(
        kern,
        grid_spec=pltpu.PrefetchScalarGridSpec(
            num_scalar_prefetch=1,
            grid=(n_seq, n_blk // SAMPLE_BLOCKS_PER_STEP),
            in_specs=[tok, tok, tok] + pages + pages + [
                      pl.BlockSpec(bias_tab.shape, lambda b, n, pt: (0, 0, 0)),
                      pl.BlockSpec(bias_own.shape, lambda b, n, pt: (0, 0))],
            out_specs=tok,
            scratch_shapes=[pltpu.VMEM((n_row, W_BR), BF16),
                            pltpu.VMEM((N_HEAD, D_HEAD, 128), F32), pltpu.VMEM((n_row, 128), F32),
                            pltpu.VMEM((n_row, 128), F32), pltpu.VMEM((n_blk, n_row, D_HEAD), F32)]),
        out_shape=jax.ShapeDtypeStruct((n_seq, t_len, W_BR), F32),
        compiler_params=_cparams(2),
    )(page_table, q, k_new, v_new, *([cache_k] * n_pg), *([cache_v] * n_pg), bias_tab, bias_own)


SOLVE_BASE = 4

def _unit_lower_inverse(a, size):
    n = a.shape[-1]
    ri = lax.broadcasted_iota(jnp.int32, (n, n), 0)
    ci = lax.broadcasted_iota(jnp.int32, (n, n), 1)
    eye = (ri == ci).astype(F32)

    def same_block(s):
        return (ri // s) == (ci // s)

    d = jnp.where(same_block(SOLVE_BASE), a, 0.0)
    inv = _mm3(eye - d, eye + _mm3(d, d))
    s = SOLVE_BASE
    while s < size:
        off = jnp.where(same_block(2 * s) & jnp.logical_not(same_block(s)), a, 0.0)
        inv = inv - _dot(_dot(inv, off), inv)
        s *= 2
    return inv


def _gdn_prompt_kernel(q_ref, k_ref, v_ref, beta_ref, gcol_ref, grow_ref, o_ref, s_ref, state_ref,
                       *, chunks):
    @pl.when(pl.program_id(1) == 0)
    def _():
        state_ref[...] = jnp.zeros(state_ref.shape, F32)

    ri = lax.broadcasted_iota(jnp.int32, (DN_CHUNK, DN_CHUNK), 0)
    ci = lax.broadcasted_iota(jnp.int32, (DN_CHUNK, DN_CHUNK), 1)
    tril = ri >= ci
    strict = ri > ci
    lower = tril.astype(F32)

    def split(x, r0):
        return [x[r0:r0 + DN_CHUNK, h * D_HEAD:(h + 1) * D_HEAD] for h in range(N_HEAD)]

    qa = q_ref[0] * (D_HEAD ** -0.5)
    ka, va, beta, gcol = k_ref[0], v_ref[0], beta_ref[0], gcol_ref[0]
    q, k, v, kt, gc, gr, b = [], [], [], [], [], [], []
    for c in range(chunks):
        r0 = c * DN_CHUNK
        q += split(qa, r0)
        k += split(ka, r0)
        v += split(va, r0)
        kt.append(ka[r0:r0 + DN_CHUNK].T.reshape(N_HEAD, D_HEAD, DN_CHUNK))
        gcum_col = jnp.dot(lower, gcol[r0:r0 + DN_CHUNK], precision=HIGHEST,
                           preferred_element_type=F32)
        gcum_row = lax.dot_general(grow_ref[c], lower, _NT, precision=HIGHEST,
                                   preferred_element_type=F32)
        gc += [gcum_col[:, h:h + 1] for h in range(N_HEAD)]
        gr += [gcum_row[h:h + 1, :] for h in range(N_HEAD)]
        b += [beta[r0:r0 + DN_CHUNK, h:h + 1] for h in range(N_HEAD)]
    q, k, v = jnp.stack(q, axis=0), jnp.stack(k, axis=0), jnp.stack(v, axis=0)
    kt = jnp.concatenate(kt, axis=0)
    gc, gr, b = jnp.stack(gc, axis=0), jnp.stack(gr, axis=0), jnp.stack(b, axis=0)
    decay = jnp.where(tril, jnp.exp(jnp.where(tril, gc - gr, 0.0)), 0.0)
    eg = jnp.exp(gc)
    glast = gr[:, :, DN_CHUNK - 1:DN_CHUNK]
    kb = k * b
    a = jnp.where(strict, _dot(kb, kt) * decay, 0.0)
    intra = _dot(q, kt) * decay
    x = _dot(_unit_lower_inverse(a, DN_CHUNK), jnp.concatenate([v * b, kb * eg], axis=2))
    u, w = x[:, :, :D_HEAD], x[:, :, D_HEAD:]
    q_dec = q * eg
    kt_dec = kt * jnp.exp(glast - gr)
    s_decay = jnp.exp(glast)

    s = state_ref[...]
    rows = []
    for c in range(chunks):
        sl = slice(c * N_HEAD, (c + 1) * N_HEAD)
        v_new = u[sl] - _dot(w[sl], s)
        o = _dot(q_dec[sl], s) + _dot(intra[sl], v_new)
        s = s * s_decay[sl] + _dot(kt_dec[sl], v_new)
        rows.append(jnp.concatenate([o[h] for h in range(N_HEAD)], axis=1))
    o_ref[0] = jnp.concatenate(rows, axis=0)
    state_ref[...] = s
    s_ref[0] = s


def _gdn_prompt_call(qc, kc, vc, beta, gcol, grow, n_seq):
    n_tok = qc.shape[0]
    t_len = n_tok // n_seq
    rows = TM
    chunks = rows // DN_CHUNK
    n_tiles = t_len // rows

    def tok(width):
        return pl.BlockSpec((1, rows, width), lambda b, t: (b, t, 0))

    o, s = pl.pallas_call(
        functools.partial(_gdn_prompt_kernel, chunks=chunks),
        grid=(n_seq, n_tiles),
        in_specs=[tok(W_BR), tok(W_BR), tok(W_BR), tok(N_HEAD), tok(N_HEAD),
                  pl.BlockSpec((chunks, N_HEAD, DN_CHUNK), lambda b, t: (b * n_tiles + t, 0, 0))],
        out_specs=[tok(W_BR), pl.BlockSpec((1, N_HEAD, D_HEAD, D_HEAD), lambda b, t: (b, 0, 0, 0))],
        out_shape=[jax.ShapeDtypeStruct((n_seq, t_len, W_BR), F32),
                   jax.ShapeDtypeStruct((n_seq, N_HEAD, D_HEAD, D_HEAD), F32)],
        scratch_shapes=[pltpu.VMEM((N_HEAD, D_HEAD, D_HEAD), F32)],
        compiler_params=_cparams(2),
    )(qc.reshape(n_seq, t_len, W_BR), kc.reshape(n_seq, t_len, W_BR), vc.reshape(n_seq, t_len, W_BR),
      beta.reshape(n_seq, t_len, N_HEAD), gcol.reshape(n_seq, t_len, N_HEAD), grow)
    return o.reshape(n_tok, W_BR), s


def _gdn_sample_kernel(q_ref, k_ref, v_ref, bcol_ref, gcol_ref, grow_ref, s0_ref, o_ref, s_ref,
                       *, t_len):
    n_row = N_HEAD * t_len
    ri = lax.broadcasted_iota(jnp.int32, (n_row, n_row), 0)
    ci = lax.broadcasted_iota(jnp.int32, (n_row, n_row), 1)
    same = (ri // t_len) == (ci // t_len)
    tril = same & (ri >= ci)
    strict = same & (ri > ci)
    q = _stack_heads(q_ref[0]) * (D_HEAD ** -0.5)
    k = _stack_heads(k_ref[0])
    v = _stack_heads(v_ref[0])
    b = bcol_ref[0]
    g_lane = jnp.broadcast_to(gcol_ref[0], (n_row, 128))
    gc = jnp.dot(tril.astype(F32), g_lane, precision=HIGHEST, preferred_element_type=F32)[:, 0:1]
    gl = jnp.dot(same.astype(F32), g_lane, precision=HIGHEST, preferred_element_type=F32)[:, 0:1]
    g_sub = jnp.broadcast_to(grow_ref[0], (SUBLANE, n_row))
    gr = lax.dot_general(g_sub, tril.astype(F32), _NT, precision=HIGHEST,
                         preferred_element_type=F32)[0:1, :]
    decay = jnp.where(tril, jnp.exp(jnp.where(tril, gc - gr, 0.0)), 0.0)
    eg = jnp.exp(gc)
    kb = k * b
    a = jnp.where(strict, lax.dot_general(kb, k, _NT, preferred_element_type=F32) * decay, 0.0)
    intra = lax.dot_general(q, k, _NT, preferred_element_type=F32) * decay
    vb = v * b
    vb_c = vb[:, 0:D_HEAD]
    for h in range(1, N_HEAD):
        vb_c = vb_c + vb[:, h * D_HEAD:(h + 1) * D_HEAD]
    x = _mm3(_unit_lower_inverse(a, t_len), jnp.concatenate([kb * eg, vb_c], axis=1))
    w, u = x[:, :W_BR], x[:, W_BR:]
    s0 = s0_ref[0]
    v_new = u - jnp.dot(w, s0, preferred_element_type=F32)
    o = jnp.dot(q * eg, s0, preferred_element_type=F32) + jnp.dot(intra, v_new, preferred_element_type=F32)
    er = lax.broadcasted_iota(jnp.int32, (W_BR, n_row), 0)
    ec = lax.broadcasted_iota(jnp.int32, (W_BR, n_row), 1)
    pick = ((er // D_HEAD) == (ec // t_len)) & ((ec % t_len) == t_len - 1)
    gl_state = jnp.dot(pick.astype(F32), jnp.broadcast_to(gc, (n_row, 128)), precision=HIGHEST,
                       preferred_element_type=F32)[:, 0:1]
    s_ref[0] = s0 * jnp.exp(gl_state) + lax.dot_general(
        k * jnp.exp(gl - gc), v_new, _TN, preferred_element_type=F32)
    o_ref[0] = _unstack_heads(jnp.concatenate([o] * N_HEAD, axis=1), t_len)


def _gdn_sample_call(qc, kc, vc, bcol, gcol, grow, s0):
    n_seq, t_len, _ = qc.shape
    n_row = N_HEAD * t_len
    assert t_len >= SOLVE_BASE and t_len & (t_len - 1) == 0
    tok = pl.BlockSpec((1, t_len, W_BR), lambda b: (b, 0, 0))
    colv = pl.BlockSpec((1, n_row, 1), lambda b: (b, 0, 0))
    st = pl.BlockSpec((1, W_BR, D_HEAD), lambda b: (b, 0, 0))
    return pl.pallas_call(
        functools.partial(_gdn_sample_kernel, t_len=t_len),
        grid=(n_seq,),
        in_specs=[tok, tok, tok, colv, colv, pl.BlockSpec((1, 1, n_row), lambda b: (b, 0, 0)), st],
        out_specs=[tok, st],
        out_shape=[jax.ShapeDtypeStruct((n_seq, t_len, W_BR), F32),
                   jax.ShapeDtypeStruct((n_seq, W_BR, D_HEAD), F32)],
        compiler_params=_cparams(1),
    )(qc, kc, vc, bcol, gcol, grow, s0)


def _out_kernel(x_ref, sc_ref, sh_ref, gate_ref, g1_ref, attn_ref, yb_ref, od_ref, og_ref, dng_ref,
                wmg_ref, woa_ref, wob_ref, woc_ref, wo_ref, gsum_ref, o_ref):
    x = x_ref[...]
    h = _mod_norm(x, g1_ref[...], sc_ref[0], sh_ref[0]).astype(BF16)
    mg = _sigmoid(jnp.dot(h, wmg_ref[...], preferred_element_type=F32))
    d = x.shape[1]
    ya = jnp.dot(attn_ref[...].astype(BF16), woa_ref[...], preferred_element_type=F32)
    yb = jnp.dot(yb_ref[...].astype(BF16), wob_ref[...], preferred_element_type=F32)
    od = od_ref[...]
    oc = (od * lax.rsqrt(_group_sumsq(od, gsum_ref[...]) * (1.0 / D_HEAD) + EPS) * dng_ref[...]) \
        * _silu(og_ref[...])
    yc = jnp.dot(oc.astype(BF16), woc_ref[...], preferred_element_type=F32)
    merged = mg[:, :d] * ya + mg[:, d:2 * d] * yb + mg[:, 2 * d:] * yc
    o_ref[...] = x + gate_ref[0] * jnp.dot(merged.astype(BF16), wo_ref[...], preferred_element_type=F32)


def _token_grid(n_tok, n_seq, prompt, d_model):
    if prompt:
        rows = TM
        n_tiles = n_tok // rows
        tiles_per_seq = n_tiles // n_seq
        mod_spec = pl.BlockSpec((1, 1, d_model), lambda t: (t // tiles_per_seq, 0, 0))
    else:
        rows, n_tiles, tiles_per_seq = n_tok, 1, 1
        mod_spec = pl.BlockSpec((1, rows, d_model), lambda t: (0, 0, 0))
    return rows, n_tiles, tiles_per_seq, mod_spec


def _out_call(x, sc, sh, gate, attn, yb, od, og, w, *, prompt, n_seq):
    n_tok, d_model = x.shape
    rows, n_tiles, _, mod_spec = _token_grid(n_tok, n_seq, prompt, d_model)

    def tok(width):
        return pl.BlockSpec((rows, width), lambda t: (t, 0))

    return pl.pallas_call(
        _out_kernel, grid=(n_tiles,),
        in_specs=[tok(d_model), mod_spec, mod_spec, mod_spec, _const_spec((1, d_model)),
                  tok(W_BR), tok(W_BR), tok(W_BR), tok(W_BR), _const_spec((1, W_BR)),
                  _const_spec(w['wmg'].shape), _const_spec(w['woa'].shape), _const_spec(w['wob'].shape),
                  _const_spec(w['woc'].shape), _const_spec(w['wo'].shape), _const_spec((W_BR, W_BR))],
        out_specs=tok(d_model),
        out_shape=jax.ShapeDtypeStruct((n_tok, d_model), F32),
        compiler_params=_cparams(1),
    )(x, sc, sh, gate, w['norm1_g'], attn, yb, od, og, w['dn_norm_g'], w['wmg'], w['woa'], w['wob'],
      w['woc'], w['wo'], w['gsum'])


def _ffn_kernel(x_ref, sc_ref, sh_ref, gate_ref, g2_ref, wg_ref, wu_ref, wd_ref, cw_ref, halo_ref,
                o_ref, tail_ref, xp_ref, *, rows, tiles_per_seq, stride, halo):
    first = (pl.program_id(0) % tiles_per_seq) == 0
    x = x_ref[...]
    h = _mod_norm(x, g2_ref[...], sc_ref[0], sh_ref[0]).astype(BF16)
    pre = jnp.dot(h, wg_ref[...], preferred_element_type=F32)
    hg, tail = _causal_conv(xp_ref, pre, cw_ref, halo_ref, first,
                            width=CONV_F, halo=halo, stride=stride, rows=rows)
    tail_ref[0] = tail
    act = _silu(hg) * jnp.dot(h, wu_ref[...], preferred_element_type=F32)
    o_ref[...] = x + gate_ref[0] * jnp.dot(act.astype(BF16), wd_ref[...], preferred_element_type=F32)


def _ffn_call(x, sc, sh, gate, w, halo_in, *, prompt, n_seq):
    n_tok, d_model = x.shape
    d_ff = w['wgate'].shape[1]
    rows, n_tiles, tiles_per_seq, mod_spec = _token_grid(n_tok, n_seq, prompt, d_model)
    stride = 1 if prompt else n_seq
    halo = SUBLANE if prompt else (CONV_F - 1) * n_seq
    n_grp = n_tiles // tiles_per_seq
    tok = pl.BlockSpec((rows, d_model), lambda t: (t, 0))
    per_seq = pl.BlockSpec((1, halo, d_ff), lambda t: (t // tiles_per_seq, 0, 0))
    kern = functools.partial(_ffn_kernel, rows=rows, tiles_per_seq=tiles_per_seq, stride=stride, halo=halo)
    return pl.pallas_call(
        kern, grid=(n_tiles,),
        in_specs=[tok, mod_spec, mod_spec, mod_spec, _const_spec((1, d_model)),
                  _const_spec(w['wgate'].shape), _const_spec(w['wup'].shape), _const_spec(w['wdown'].shape),
                  _const_spec((CONV_F, d_ff)), per_seq],
        out_specs=[tok, per_seq],
        out_shape=[jax.ShapeDtypeStruct((n_tok, d_model), F32),
                   jax.ShapeDtypeStruct((n_grp, halo, d_ff), F32)],
        scratch_shapes=[pltpu.VMEM((halo + rows, d_ff), F32)],
        compiler_params=_cparams(1),
    )(x, sc, sh, gate, w['norm2_g'], w['wgate'], w['wup'], w['wdown'], w['ffn_conv_w'], halo_in)


def _t5_bucket(rel):
    n = jnp.maximum(rel, 0)
    max_exact = REL_BUCKETS // 2
    nf = jnp.maximum(n, 1).astype(F32)
    large = max_exact + (jnp.log(nf / max_exact) / math.log(REL_MAX_DIST / max_exact)
                         * (REL_BUCKETS - max_exact)).astype(jnp.int32)
    large = jnp.minimum(large, REL_BUCKETS - 1)
    return jnp.where(n < max_exact, n, large)


def _bias_tables(rel_bias, n_blk_prompt, past_len, t_s):
    blk = MOBA_BLOCK

    def lookup(rel):
        onehot = jax.nn.one_hot(_t5_bucket(rel).reshape(-1), REL_BUCKETS, dtype=F32)
        tab = lax.dot_general(rel_bias, onehot, (((0,), (1,)), ((), ())), precision=HIGHEST)
        return jnp.where(rel.reshape(-1) >= 0, tab, NEG).reshape((N_HEAD,) + rel.shape)

    n_dist = min(n_blk_prompt, -(-(REL_MAX_DIST + blk - 1) // blk) + 1)
    d = jnp.arange(n_dist, dtype=jnp.int32)[:, None, None]
    ki = jnp.arange(blk, dtype=jnp.int32)[None, :, None]
    qj = jnp.arange(blk, dtype=jnp.int32)[None, None, :]
    tab_p = lookup(d * blk + qj - ki)
    n_blk_s = past_len // blk
    nb = jnp.arange(n_blk_s, dtype=jnp.int32)[:, None, None]
    tq = jnp.arange(t_s, dtype=jnp.int32)[None, :, None]
    ko = jnp.arange(blk, dtype=jnp.int32)[None, None, :]
    tab_s = lookup(past_len + tq - nb * blk - ko)
    tab_s = jnp.transpose(tab_s, (1, 0, 2, 3)).reshape(n_blk_s, N_HEAD * t_s, blk)
    rel_o = jnp.arange(t_s, dtype=jnp.int32)[:, None] - jnp.arange(128, dtype=jnp.int32)[None, :]
    tab_o = lookup(rel_o).reshape(N_HEAD * t_s, 128)
    return tab_p.astype(F32), tab_s.astype(F32), tab_o.astype(F32)


def _layer_weights(l, p, gsum):
    w_in = p['w_in'][l]
    cuts = [0, 3 * W_BR, 6 * W_BR, 10 * W_BR, 10 * W_BR + 2 * N_HEAD]
    wba = w_in[:, cuts[3]:cuts[4]].astype(BF16)
    row = lambda a: a.reshape(1, -1)
    return {
        'wa': w_in[:, cuts[0]:cuts[1]].astype(BF16),
        'wb': w_in[:, cuts[1]:cuts[2]].astype(BF16),
        'wc': w_in[:, cuts[2]:cuts[3]].astype(BF16),
        'wba': wba, 'wbat': wba.T,
        'wmg': w_in[:, cuts[4]:].astype(BF16),
        'norm1_g': row(p['norm1_g'][l]), 'norm2_g': row(p['norm2_g'][l]),
        'qn_g': row(jnp.tile(p['qn_g'][l], N_HEAD)), 'kn_g': row(jnp.tile(p['kn_g'][l], N_HEAD)),
        'dn_norm_g': row(jnp.tile(p['dn_norm_g'][l], N_HEAD)),
        'conv_b_w': p['conv_b_w'][l], 'dn_conv_w': p['dn_conv_w'][l], 'ffn_conv_w': p['ffn_conv_w'][l],
        'a_log_row': row(p['dn_a_log'][l]), 'dt_bias_row': row(p['dn_dt_bias'][l]),
        'a_log_col': p['dn_a_log'][l].reshape(-1, 1), 'dt_bias_col': p['dn_dt_bias'][l].reshape(-1, 1),
        'woa': p['w_oa'][l].astype(BF16), 'wob': p['w_ob'][l].astype(BF16), 'woc': p['w_oc'][l].astype(BF16),
        'wo': p['w_o'][l].astype(BF16),
        'wgate': p['w_gate'][l].astype(BF16), 'wup': p['w_up'][l].astype(BF16),
        'wdown': p['w_down'][l].astype(BF16),
        'gsum': gsum,
    }


def kernel(x_prompt, x_sample, cache_k, cache_v, state_conv_b, state_conv_dn, state_dn, state_conv_ffn,
           page_table, c_prompt, c_sample, rel_bias, w_ada, b_ada, norm1_g, norm2_g, w_in, qn_g, kn_g,
           conv_b_w, dn_conv_w, dn_a_log, dn_dt_bias, dn_norm_g, w_oa, w_ob, w_oc, w_o, w_gate, w_up,
           ffn_conv_w, w_down):
    bp, tp, d_model = x_prompt.shape
    bs, ts, _ = x_sample.shape
    depth = w_in.shape[0]
    d_ff = w_gate.shape[2]
    n_pool, page = cache_k.shape[1], cache_k.shape[2]
    past_len = page_table.shape[1] * page
    assert tp % MOBA_BLOCK == 0 and past_len % MOBA_BLOCK == 0 and MOBA_BLOCK == 2 * page
    assert ts <= 128 and past_len // MOBA_BLOCK >= MOBA_TOPK and (bs * ts) % SUBLANE == 0
    p = dict(w_in=w_in, norm1_g=norm1_g, norm2_g=norm2_g, qn_g=qn_g, kn_g=kn_g, conv_b_w=conv_b_w,
             dn_conv_w=dn_conv_w, dn_a_log=dn_a_log, dn_dt_bias=dn_dt_bias, dn_norm_g=dn_norm_g,
             w_oa=w_oa, w_ob=w_ob, w_oc=w_oc, w_o=w_o, w_gate=w_gate, w_up=w_up,
             ffn_conv_w=ffn_conv_w, w_down=w_down)

    lane = jnp.arange(W_BR) // D_HEAD
    gsum = (lane[:, None] == lane[None, :]).astype(BF16)
    tab_p, tab_s, tab_o = _bias_tables(rel_bias, tp // MOBA_BLOCK, past_len, ts)

    n_c = bp + bs
    c_rows = -(-n_c // SUBLANE) * SUBLANE
    c_all = jnp.concatenate([c_prompt, c_sample, jnp.zeros((c_rows - n_c, d_model), F32)], axis=0)
    mods = _ada_call(c_all, w_ada, b_ada)

    ck = jnp.transpose(cache_k, (0, 1, 3, 4, 2))
    cv = jnp.transpose(cache_v, (0, 1, 3, 4, 2))

    xp = x_prompt.reshape(bp * tp, d_model)
    xs = jnp.transpose(x_sample, (1, 0, 2)).reshape(ts * bs, d_model)

    def to_tm(a):
        return jnp.transpose(a, (1, 0, 2)).reshape(1, a.shape[1] * bs, a.shape[2])

    def from_tm(a, r):
        return jnp.transpose(a.reshape(r, bs, a.shape[-1]), (1, 0, 2))

    zero_b = jnp.zeros((bp, SUBLANE, W_BR), F32)
    zero_c = jnp.zeros((bp, SUBLANE, 3 * W_BR), F32)
    zero_f = jnp.zeros((bp, SUBLANE, d_ff), F32)

    outs = [[] for _ in range(12)]
    for l in range(depth):
        w = _layer_weights(l, p, gsum)
        mod_p = [m.reshape(bp, 1, d_model) for m in jnp.split(mods[l, :bp], 6, axis=-1)]
        mod_s = [jnp.tile(m, (ts, 1)).reshape(1, ts * bs, d_model)
                 for m in jnp.split(mods[l, bp:bp + bs], 6, axis=-1)]

        sh1, sc1, g1, sh2, sc2, g2 = mod_p
        (q, k, v, yb, tail_b, qc, kc, vc, og, beta, gcol, tail_c, k_bf, vt_bf, kmean, grow) = _in_call(
            xp, sc1, sh1, w, prompt=True, n_seq=bp, halo_b_in=zero_b, halo_c_in=zero_c)
        attn = _attn_prompt_call(q, k_bf, vt_bf, kmean, tab_p, bp)
        od, s_p = _gdn_prompt_call(qc, kc, vc, beta, gcol, grow, bp)
        xp = _out_call(xp, sc1, sh1, g1, attn, yb, od, og, w, prompt=True, n_seq=bp)
        xp, tail_f = _ffn_call(xp, sc2, sh2, g2, w, zero_f, prompt=True, n_seq=bp)
        outs[0].append(k.reshape(bp, tp, N_HEAD, D_HEAD))
        outs[1].append(v.reshape(bp, tp, N_HEAD, D_HEAD))
        outs[4].append(tail_b[:, SUBLANE - (CONV_B - 1):])
        outs[6].append(tail_c[:, SUBLANE - (CONV_C - 1):])
        outs[8].append(s_p)
        outs[10].append(tail_f[:, SUBLANE - (CONV_F - 1):])

        sh1, sc1, g1, sh2, sc2, g2 = mod_s
        (q, k, v, yb, tail_b, qc, kc, vc, og, beta, gcol, tail_c) = _in_call(
            xs, sc1, sh1, w, prompt=False, n_seq=bs,
            halo_b_in=to_tm(state_conv_b[l]), halo_c_in=to_tm(state_conv_dn[l]))
        q_b, k_b, v_b = from_tm(q, ts), from_tm(k, ts), from_tm(v, ts)
        attn = _attn_sample_call(page_table, q_b, k_b, v_b, ck, cv, l, tab_s, tab_o)
        beta_ht = jnp.transpose(beta.reshape(ts, bs, N_HEAD), (1, 2, 0)).reshape(bs, N_HEAD * ts)
        g_ht = jnp.transpose(gcol.reshape(ts, bs, N_HEAD), (1, 2, 0)).reshape(bs, N_HEAD * ts)
        od, s_s = _gdn_sample_call(from_tm(qc, ts), from_tm(kc, ts), from_tm(vc, ts),
                                   beta_ht[:, :, None], g_ht[:, :, None], g_ht[:, None, :],
                                   state_dn[l].reshape(bs, W_BR, D_HEAD))
        attn_tm = jnp.transpose(attn, (1, 0, 2)).reshape(ts * bs, W_BR)
        od_tm = jnp.transpose(od, (1, 0, 2)).reshape(ts * bs, W_BR)
        xs = _out_call(xs, sc1, sh1, g1, attn_tm, yb, od_tm, og, w, prompt=False, n_seq=bs)
        xs, tail_f = _ffn_call(xs, sc2, sh2, g2, w, to_tm(state_conv_ffn[l]), prompt=False, n_seq=bs)
        outs[2].append(k_b.reshape(bs, ts, N_HEAD, D_HEAD))
        outs[3].append(v_b.reshape(bs, ts, N_HEAD, D_HEAD))
        outs[5].append(from_tm(tail_b[0], CONV_B - 1))
        outs[7].append(from_tm(tail_c[0], CONV_C - 1))
        outs[9].append(s_s.reshape(bs, N_HEAD, D_HEAD, D_HEAD))
        outs[11].append(from_tm(tail_f[0], CONV_F - 1))

    y_p = xp.reshape(bp, tp, d_model)
    y_s = from_tm(xs, ts)
    return (y_p, y_s) + tuple(jnp.stack(o) for o in outs)
```

```python
import functools
import math

import jax
import jax.numpy as jnp
from jax import lax
from jax.experimental import pallas as pl
from jax.experimental.pallas import tpu as pltpu

F32 = jnp.float32
BF16 = jnp.bfloat16

N_HEAD = 8
D_HEAD = 64
W_BR = N_HEAD * D_HEAD
MOBA_BLOCK = 256
MOBA_TOPK = 3
CONV_B = 3
CONV_C = 4
CONV_F = 3
DN_CHUNK = 64
REL_BUCKETS = 32
REL_MAX_DIST = 2048
N_BRANCH = 3
EPS = 1e-6
NEG = -1e30

TM = MOBA_BLOCK
SUBLANE = 8
VMEM_LIMIT = 56 * 1024 * 1024
HIGHEST = lax.Precision.HIGHEST

_NT = (((1,), (1,)), ((), ()))
_TN = (((0,), (0,)), ((), ()))


def _const_spec(shape):
    nd = len(shape)
    return pl.BlockSpec(shape, lambda *_: (0,) * nd, pipeline_mode=pl.Buffered(1))


def _cparams(n_grid):
    return pltpu.CompilerParams(dimension_semantics=("arbitrary",) * n_grid,
                                vmem_limit_bytes=VMEM_LIMIT)


def _sigmoid(x):
    return 1.0 / (1.0 + jnp.exp(-x))


def _silu(x):
    return x * _sigmoid(x)


def _softplus(x):
    return jnp.maximum(x, 0.0) + jnp.log1p(jnp.exp(-jnp.abs(x)))


def _split_dot(a, b_bf16):
    hi = a.astype(BF16)
    lo = (a - hi.astype(F32)).astype(BF16)
    return (jnp.dot(hi, b_bf16, preferred_element_type=F32)
            + jnp.dot(lo, b_bf16, preferred_element_type=F32))


def _dot(a, b):
    if a.ndim == 3:
        return lax.dot_general(a, b, (((2,), (1,)), ((0,), (0,))), preferred_element_type=F32)
    return jnp.dot(a, b, preferred_element_type=F32)


def _mm3(a, b):
    ah = a.astype(BF16)
    al = (a - ah.astype(F32)).astype(BF16)
    bh = b.astype(BF16)
    bl = (b - bh.astype(F32)).astype(BF16)
    return _dot(ah, bh) + _dot(ah, bl) + _dot(al, bh)


def _group_sumsq(y, gsum_bf16):
    return _split_dot(y * y, gsum_bf16)


def _mod_norm(x, gain, sc, sh):
    ms = jnp.mean(x * x, axis=-1, keepdims=True)
    return (x * lax.rsqrt(ms + EPS) * gain) * (1.0 + sc) + sh


def _causal_conv(xp_ref, cur, w_ref, halo_ref, first, *, width, halo, stride, rows):
    @pl.when(first)
    def _():
        xp_ref[0:halo, :] = halo_ref[0]

    xp_ref[halo:halo + rows, :] = cur
    off = halo - (width - 1) * stride
    y = xp_ref[off:off + rows, :] * w_ref[0:1, :]
    for i in range(1, width):
        off = halo - (width - 1 - i) * stride
        y = y + xp_ref[off:off + rows, :] * w_ref[i:i + 1, :]
    tail = xp_ref[rows:rows + halo, :]
    xp_ref[0:halo, :] = tail
    return y, tail


def _ada_kernel(c_ref, w_ref, b_ref, o_ref):
    c = _silu(c_ref[...]).astype(BF16)
    o_ref[0] = jnp.dot(c, w_ref[0].astype(BF16), preferred_element_type=F32) + b_ref[0]


def _ada_call(c_all, w_ada, b_ada):
    depth, d_model, n_mod = w_ada.shape
    rows = c_all.shape[0]
    bn = 1536
    return pl.pallas_call(
        _ada_kernel,
        grid=(depth, n_mod // bn),
        in_specs=[pl.BlockSpec((rows, d_model), lambda l, j: (0, 0)),
                  pl.BlockSpec((1, d_model, bn), lambda l, j: (l, 0, j)),
                  pl.BlockSpec((1, 1, bn), lambda l, j: (l, 0, j))],
        out_specs=pl.BlockSpec((1, rows, bn), lambda l, j: (l, 0, j)),
        out_shape=jax.ShapeDtypeStruct((depth, rows, n_mod), F32),
        compiler_params=_cparams(2),
    )(c_all, w_ada, b_ada.reshape(depth, 1, n_mod))


def _in_kernel(*refs, prompt, rows, tiles_per_seq, stride, halo_b, halo_c):
    (x_ref, sc_ref, sh_ref, g1_ref, wa_ref, wb_ref, wc_ref, wba_ref, wbat_ref, qn_ref, kn_ref,
     cbw_ref, cdw_ref, alog_ref, dtb_ref, alogc_ref, dtbc_ref, gsum_ref, hb_ref, hc_ref) = refs[:20]
    outs = refs[20:-2]
    xpb_ref, xpc_ref = refs[-2:]
    if prompt:
        (q_ref, k_ref, v_ref, yb_ref, tailb_ref, qc_ref, kc_ref, vc_ref, og_ref, beta_ref,
         gcol_ref, tailc_ref, kbf_ref, vt_ref, kmean_ref, grow_ref) = outs
    else:
        (q_ref, k_ref, v_ref, yb_ref, tailb_ref, qc_ref, kc_ref, vc_ref, og_ref, beta_ref,
         gcol_ref, tailc_ref) = outs

    first = (pl.program_id(0) % tiles_per_seq) == 0
    gsum = gsum_ref[...]
    h = _mod_norm(x_ref[...], g1_ref[...], sc_ref[0], sh_ref[0])
    hb16 = h.astype(BF16)

    za = jnp.dot(hb16, wa_ref[...], preferred_element_type=F32)
    qa, ka, va = za[:, :W_BR], za[:, W_BR:2 * W_BR], za[:, 2 * W_BR:]
    q = (qa * lax.rsqrt(_group_sumsq(qa, gsum) * (1.0 / D_HEAD) + EPS) * qn_ref[...]) * (D_HEAD ** -0.5)
    k = ka * lax.rsqrt(_group_sumsq(ka, gsum) * (1.0 / D_HEAD) + EPS) * kn_ref[...]
    q_ref[...] = q
    k_ref[...] = k
    v_ref[...] = va
    if prompt:
        kbf_ref[0] = k.astype(BF16)
        vt_ref[0] = va.T.astype(BF16)
        kmean_ref[0] = jnp.sum(k, axis=0, keepdims=True) * (1.0 / MOBA_BLOCK)

    zb = jnp.dot(hb16, wb_ref[...], preferred_element_type=F32)
    hb, bg, cg = zb[:, :W_BR], zb[:, W_BR:2 * W_BR], zb[:, 2 * W_BR:]
    uc, tail_b = _causal_conv(xpb_ref, cg * hb, cbw_ref, hb_ref, first,
                              width=CONV_B, halo=halo_b, stride=stride, rows=rows)
    yb_ref[...] = bg * uc
    tailb_ref[0] = tail_b

    zc = jnp.dot(hb16, wc_ref[...], preferred_element_type=F32)
    qkv, tail_c = _causal_conv(xpc_ref, zc[:, :3 * W_BR], cdw_ref, hc_ref, first,
                               width=CONV_C, halo=halo_c, stride=stride, rows=rows)
    tailc_ref[0] = tail_c
    og_ref[...] = zc[:, 3 * W_BR:]
    qkv = _silu(qkv)
    qc, kc, vc = qkv[:, :W_BR], qkv[:, W_BR:2 * W_BR], qkv[:, 2 * W_BR:]
    qc_ref[...] = qc * lax.rsqrt(_group_sumsq(qc, gsum) + EPS)
    kc_ref[...] = kc * lax.rsqrt(_group_sumsq(kc, gsum) + EPS)
    vc_ref[...] = vc
    zba = jnp.dot(hb16, wba_ref[...], preferred_element_type=F32)
    beta_ref[...] = _sigmoid(zba[:, :N_HEAD])
    gcol_ref[...] = -jnp.exp(alog_ref[...]) * _softplus(zba[:, N_HEAD:] + dtb_ref[...])
    if prompt:
        zbat = lax.dot_general(wbat_ref[...], hb16, _NT, preferred_element_type=F32)
        g_row = -jnp.exp(alogc_ref[...]) * _softplus(zbat[N_HEAD:, :] + dtbc_ref[...])
        for j in range(rows // DN_CHUNK):
            grow_ref[j] = g_row[:, j * DN_CHUNK:(j + 1) * DN_CHUNK]


def _in_call(x, sc, sh, w, *, prompt, n_seq, halo_b_in, halo_c_in):
    n_tok, d_model = x.shape
    if prompt:
        rows, stride = TM, 1
        n_tiles = n_tok // rows
        tiles_per_seq = n_tiles // n_seq
        halo_b = halo_c = SUBLANE
        mod_spec = pl.BlockSpec((1, 1, d_model), lambda t: (t // tiles_per_seq, 0, 0))
    else:
        rows, stride = n_tok, n_seq
        n_tiles, tiles_per_seq = 1, 1
        halo_b, halo_c = (CONV_B - 1) * n_seq, (CONV_C - 1) * n_seq
        mod_spec = pl.BlockSpec((1, rows, d_model), lambda t: (0, 0, 0))
    n_grp = n_tiles // tiles_per_seq

    def tok(width):
        return pl.BlockSpec((rows, width), lambda t: (t, 0))

    def per_seq(r, width):
        return pl.BlockSpec((1, r, width), lambda t: (t // tiles_per_seq, 0, 0))

    in_specs = [tok(d_model), mod_spec, mod_spec, _const_spec((1, d_model)),
                _const_spec(w['wa'].shape), _const_spec(w['wb'].shape), _const_spec(w['wc'].shape),
                _const_spec(w['wba'].shape), _const_spec(w['wbat'].shape),
                _const_spec((1, W_BR)), _const_spec((1, W_BR)),
                _const_spec((CONV_B, W_BR)), _const_spec((CONV_C, 3 * W_BR)),
                _const_spec((1, N_HEAD)), _const_spec((1, N_HEAD)),
                _const_spec((N_HEAD, 1)), _const_spec((N_HEAD, 1)),
                _const_spec((W_BR, W_BR)),
                per_seq(halo_b, W_BR), per_seq(halo_c, 3 * W_BR)]
    out_specs = [tok(W_BR), tok(W_BR), tok(W_BR), tok(W_BR), per_seq(halo_b, W_BR),
                 tok(W_BR), tok(W_BR), tok(W_BR), tok(W_BR), tok(N_HEAD), tok(N_HEAD),
                 per_seq(halo_c, 3 * W_BR)]
    out_shape = [jax.ShapeDtypeStruct((n_tok, W_BR), F32)] * 4 + [
        jax.ShapeDtypeStruct((n_grp, halo_b, W_BR), F32)] + [
        jax.ShapeDtypeStruct((n_tok, W_BR), F32)] * 4 + [
        jax.ShapeDtypeStruct((n_tok, N_HEAD), F32)] * 2 + [
        jax.ShapeDtypeStruct((n_grp, halo_c, 3 * W_BR), F32)]
    if prompt:
        cpt = rows // DN_CHUNK
        out_specs += [pl.BlockSpec((1, rows, W_BR), lambda t: (t, 0, 0)),
                      pl.BlockSpec((1, W_BR, rows), lambda t: (t, 0, 0)),
                      pl.BlockSpec((1, 1, W_BR), lambda t: (t, 0, 0)),
                      pl.BlockSpec((cpt, N_HEAD, DN_CHUNK), lambda t: (t, 0, 0))]
        out_shape += [jax.ShapeDtypeStruct((n_tiles, rows, W_BR), BF16),
                      jax.ShapeDtypeStruct((n_tiles, W_BR, rows), BF16),
                      jax.ShapeDtypeStruct((n_tiles, 1, W_BR), F32),
                      jax.ShapeDtypeStruct((n_tiles * cpt, N_HEAD, DN_CHUNK), F32)]
    kern = functools.partial(_in_kernel, prompt=prompt, rows=rows, tiles_per_seq=tiles_per_seq,
                             stride=stride, halo_b=halo_b, halo_c=halo_c)
    return pl.pallas_call(
        kern, grid=(n_tiles,), in_specs=in_specs, out_specs=out_specs, out_shape=out_shape,
        scratch_shapes=[pltpu.VMEM((halo_b + rows, W_BR), F32),
                        pltpu.VMEM((halo_c + rows, 3 * W_BR), F32)],
        compiler_params=_cparams(1),
    )(x, sc, sh, w['norm1_g'], w['wa'], w['wb'], w['wc'], w['wba'], w['wbat'], w['qn_g'], w['kn_g'],
      w['conv_b_w'], w['dn_conv_w'], w['a_log_row'], w['dt_bias_row'], w['a_log_col'],
      w['dt_bias_col'], w['gsum'], halo_b_in, halo_c_in)


def _gate_scores(a, b):
    return lax.dot_general(a.astype(BF16), b.astype(BF16), _NT, preferred_element_type=F32)


def _top3_mask_cols(gates, own, n_blk):
    blk = lax.broadcasted_iota(jnp.int32, gates.shape, 0)
    g = jnp.where(blk < own, gates, NEG)
    mask = jnp.full(gates.shape, NEG, F32)
    for _ in range(MOBA_TOPK):
        mx = jnp.max(g, axis=0, keepdims=True)
        idx = jnp.min(jnp.where(g == mx, blk, n_blk), axis=0, keepdims=True)
        hit = blk == idx
        mask = jnp.where(hit & (idx < own), 0.0, mask)
        g = jnp.where(hit, -jnp.inf, g)
    return mask


def _attn_prompt_kernel(q_ref, k_ref, vt_ref, kmean_ref, bias_ref, o_ref, sel_ref, s_ref, p_ref,
                        *, n_blk, n_dist):
    qt = pl.program_id(2)
    qf = q_ref[0]
    lane = lax.broadcasted_iota(jnp.int32, qf.shape, 1)
    ones = jnp.ones((16, MOBA_BLOCK), BF16)
    qbs = []
    for hh in range(2):
        qh = jnp.where(lane // D_HEAD == hh, qf, 0.0)
        sel_ref[hh] = _top3_mask_cols(_gate_scores(kmean_ref[0], qh), qt, n_blk)
        qbs.append(qh.astype(BF16))

    def lhs_v(hh, n):
        return jnp.concatenate([vt_ref[0, n, hh * D_HEAD:(hh + 1) * D_HEAD, :], ones], axis=0)

    def scores(hh, n):
        return lax.dot_general(k_ref[0, n], qbs[hh], _NT, preferred_element_type=F32)

    def masked_scores(hh, n):
        d = jnp.minimum(qt - n, n_dist - 1)
        return scores(hh, n) + bias_ref[hh, d] + sel_ref[hh, pl.ds(n, 1), :]

    carry = []
    for hh in range(2):
        s = scores(hh, qt) + bias_ref[hh, 0]
        m = jnp.max(s, axis=0, keepdims=True)
        p_ref[1, hh] = jnp.exp(s - m).astype(BF16)
        carry.append((m, jnp.ones_like(m), jnp.zeros((D_HEAD + 16, TM), F32)))
        s_ref[0, hh] = masked_scores(hh, 0)
    carry = tuple(carry)

    def pending_pv(hh, n, slot):
        v_blk = jnp.where(n == 0, qt, n - 1)
        return jnp.dot(lhs_v(hh, v_blk), p_ref[1 - slot, hh], preferred_element_type=F32)

    def step(n, carry):
        slot = n % 2
        nxt = [masked_scores(hh, n + 1) for hh in range(2)]
        pv = [pending_pv(hh, n, slot) for hh in range(2)]
        out = []
        for hh in range(2):
            m, alpha, acc = carry[hh]
            s = s_ref[slot, hh]
            m_new = jnp.maximum(m, jnp.max(s, axis=0, keepdims=True))
            p_ref[slot, hh] = jnp.exp(s - m_new).astype(BF16)
            out.append((m_new, jnp.exp(m - m_new), alpha * acc + pv[hh]))
        for hh in range(2):
            s_ref[1 - slot, hh] = nxt[hh]
        return tuple(out)

    carry = lax.fori_loop(0, qt, step, carry)
    outs = []
    for hh in range(2):
        _, alpha, acc = carry[hh]
        acc = alpha * acc + pending_pv(hh, qt, qt % 2)
        outs.append(acc[:D_HEAD] / acc[D_HEAD:D_HEAD + 1])
    o_ref[0] = jnp.concatenate(outs, axis=0).T


def _attn_prompt_call(q, k_bf, vt_bf, kmean, bias_tab, n_seq):
    n_tok = q.shape[0]
    t_len = n_tok // n_seq
    n_blk = t_len // MOBA_BLOCK
    n_dist = bias_tab.shape[1]
    q3 = q.reshape(n_seq, t_len, W_BR)
    k4 = k_bf.reshape(n_seq, n_blk, MOBA_BLOCK, W_BR)
    vt4 = vt_bf.reshape(n_seq, n_blk, W_BR, MOBA_BLOCK)
    km3 = kmean.reshape(n_seq, n_blk, W_BR)
    kern = functools.partial(_attn_prompt_kernel, n_blk=n_blk, n_dist=n_dist)
    out = pl.pallas_call(
        kern,
        grid=(n_seq, N_HEAD // 2, n_blk),
        in_specs=[pl.BlockSpec((1, TM, 128), lambda b, hp, t: (b, t, hp)),
                  pl.BlockSpec((1, n_blk, MOBA_BLOCK, 128), lambda b, hp, t: (b, 0, 0, hp)),
                  pl.BlockSpec((1, n_blk, 128, MOBA_BLOCK), lambda b, hp, t: (b, 0, hp, 0)),
                  pl.BlockSpec((1, n_blk, 128), lambda b, hp, t: (b, 0, hp)),
                  pl.BlockSpec((2, n_dist, MOBA_BLOCK, MOBA_BLOCK), lambda b, hp, t: (hp, 0, 0, 0))],
        out_specs=pl.BlockSpec((1, TM, 128), lambda b, hp, t: (b, t, hp)),
        out_shape=jax.ShapeDtypeStruct((n_seq, t_len, W_BR), F32),
        scratch_shapes=[pltpu.VMEM((2, n_blk, TM), F32),
                        pltpu.VMEM((2, 2, MOBA_BLOCK, TM), F32),
                        pltpu.VMEM((2, 2, MOBA_BLOCK, TM), BF16)],
        compiler_params=_cparams(3),
    )(q3, k4, vt4, km3, bias_tab)
    return out.reshape(n_tok, W_BR)


def _stack_heads(x):
    t_len = x.shape[0]
    tiled = jnp.concatenate([x] * N_HEAD, axis=0)
    row = lax.broadcasted_iota(jnp.int32, tiled.shape, 0)
    lane = lax.broadcasted_iota(jnp.int32, tiled.shape, 1)
    return jnp.where(row // t_len == lane // D_HEAD, tiled, 0.0)


def _unstack_heads(y, t_len):
    row = lax.broadcasted_iota(jnp.int32, y.shape, 0)
    lane = lax.broadcasted_iota(jnp.int32, y.shape, 1)
    y = jnp.where(row // t_len == lane // D_HEAD, y, 0.0)
    out = y[0:t_len]
    for h in range(1, N_HEAD):
        out = out + y[h * t_len:(h + 1) * t_len]
    return out


_BQK = (((2,), (2,)), ((0,), (0,)))
_BPV = (((2,), (1,)), ((0,), (0,)))


def _split_heads(x):
    return jnp.stack([x[:, h * D_HEAD:(h + 1) * D_HEAD] for h in range(N_HEAD)], axis=0)


SAMPLE_BLOCKS_PER_STEP = 4
PAGES_PER_BLOCK = 2


def _attn_sample_kernel(pt_ref, q_ref, kn_ref, vn_ref, *refs, n_blk, t_len, page):
    del pt_ref
    n_pg = SAMPLE_BLOCKS_PER_STEP * PAGES_PER_BLOCK
    ck_refs, cv_refs = refs[:n_pg], refs[n_pg:2 * n_pg]
    bias_ref, bown_ref, o_ref, qbd_ref, kmean_ref, m_ref, l_ref, oacc_ref = refs[2 * n_pg:]
    step = pl.program_id(1)
    n_row = N_HEAD * t_len

    @pl.when(step == 0)
    def _():
        qbd_ref[...] = _stack_heads(q_ref[0]).astype(BF16)
        m_ref[...] = jnp.zeros(m_ref.shape, F32)
        l_ref[...] = jnp.zeros(l_ref.shape, F32)
        kmean_ref[...] = jnp.zeros(kmean_ref.shape, F32)

    def own_heads(wide):
        return jnp.concatenate([wide[h * t_len:(h + 1) * t_len, h * D_HEAD:(h + 1) * D_HEAD]
                                for h in range(N_HEAD)], axis=0)

    qbd = qbd_ref[...]
    col = lax.broadcasted_iota(jnp.int32, m_ref.shape, 1)
    kcol = lax.broadcasted_iota(jnp.int32, kmean_ref.shape, 2)
    m_all, l_all, kmean_all = m_ref[...], l_ref[...], kmean_ref[...]
    for j in range(SAMPLE_BLOCKS_PER_STEP):
        n = step * SAMPLE_BLOCKS_PER_STEP + j
        s_pg, ksum = [], None
        for i in range(PAGES_PER_BLOCK):
            kt = ck_refs[PAGES_PER_BLOCK * j + i][0, 0]
            part = jnp.sum(kt, axis=2, keepdims=True)
            ksum = part if ksum is None else ksum + part
            s_pg.append(jnp.dot(qbd, kt.reshape(W_BR, page).astype(BF16), preferred_element_type=F32))
        kmean_all = jnp.where(kcol == n, ksum * (1.0 / MOBA_BLOCK), kmean_all)
        s = jnp.concatenate(s_pg, axis=1) + bias_ref[n]
        m = jnp.max(s, axis=1, keepdims=True)
        p = jnp.exp(s - m)
        l = jnp.sum(p, axis=1, keepdims=True)
        pb = p.astype(BF16)
        pv = None
        for i in range(PAGES_PER_BLOCK):
            vt = cv_refs[PAGES_PER_BLOCK * j + i][0, 0].reshape(W_BR, page).astype(BF16)
            part = lax.dot_general(pb[:, i * page:(i + 1) * page], vt, _NT,
                                   preferred_element_type=F32)
            pv = part if pv is None else pv + part
        oacc_ref[n] = own_heads(pv)
        m_all = jnp.where(col == n, m, m_all)
        l_all = jnp.where(col == n, l, l_all)
    m_ref[...] = m_all
    l_ref[...] = l_all
    kmean_ref[...] = kmean_all

    @pl.when(step == pl.num_programs(1) - 1)
    def _():
        gates = jnp.dot(qbd, kmean_all.reshape(W_BR, 128).astype(BF16),
                        preferred_element_type=F32)[:, 0:n_blk]
        blk = lax.broadcasted_iota(jnp.int32, gates.shape, 1)
        g = gates
        sel = jnp.zeros(gates.shape, jnp.bool_)
        for _ in range(MOBA_TOPK):
            mx = jnp.max(g, axis=1, keepdims=True)
            idx = jnp.min(jnp.where(g == mx, blk, n_blk), axis=1, keepdims=True)
            hit = blk == idx
            sel = sel | hit
            g = jnp.where(hit, -jnp.inf, g)
        m_all = m_ref[:, 0:n_blk]
        l_all = l_ref[:, 0:n_blk]

        pad = jnp.zeros((128 - t_len, W_BR), F32)
        kown = jnp.concatenate([kn_ref[0], pad], axis=0).astype(BF16)
        vown = jnp.concatenate([vn_ref[0], pad], axis=0).astype(BF16)
        s_own = lax.dot_general(qbd, kown, _NT, preferred_element_type=F32) + bown_ref[...]
        m_own = jnp.max(s_own, axis=1, keepdims=True)
        m_tot = jnp.maximum(m_own, jnp.max(jnp.where(sel, m_all, -jnp.inf), axis=1, keepdims=True))
        p_own = jnp.exp(s_own - m_tot)
        wgt = jnp.where(sel, jnp.exp(m_all - m_tot), 0.0)
        denom = jnp.sum(p_own, axis=1, keepdims=True) + jnp.sum(wgt * l_all, axis=1, keepdims=True)
        acc = own_heads(jnp.dot(p_own.astype(BF16), vown, preferred_element_type=F32))
        for i in range(n_blk):
            acc = acc + wgt[:, i:i + 1] * oacc_ref[i]
        out = acc / denom
        o_ref[0] = jnp.concatenate([out[h * t_len:(h + 1) * t_len] for h in range(N_HEAD)], axis=1)


def _attn_sample_call(page_table, q, k_new, v_new, cache_k, cache_v, layer, bias_tab, bias_own):
    n_seq, t_len, _ = q.shape
    n_pages = page_table.shape[1]
    page = cache_k.shape[4]
    n_blk = n_pages * page // MOBA_BLOCK
    n_row = N_HEAD * t_len
    tok = pl.BlockSpec((1, t_len, W_BR), lambda b, n, pt: (b, 0, 0))

    n_pg = SAMPLE_BLOCKS_PER_STEP * PAGES_PER_BLOCK
    assert page * PAGES_PER_BLOCK == MOBA_BLOCK and n_blk % SAMPLE_BLOCKS_PER_STEP == 0 and n_blk <= 128

    def page_spec(j):
        return pl.BlockSpec((1, 1, N_HEAD, D_HEAD, page),
                            lambda b, n, pt: (layer, pt[b, n_pg * n + j], 0, 0, 0))

    pages = [page_spec(j) for j in range(n_pg)]
    kern = functools.partial(_attn_sample_kernel, n_blk=n_blk, t_len=t_len, page=page)
    return pl.pallas_call(
        kern,
        grid_spec=pltpu.PrefetchScalarGridSpec(
            num_scalar_prefetch=1,
            grid=(n_seq, n_blk // SAMPLE_BLOCKS_PER_STEP),
            in_specs=[tok, tok, tok] + pages + pages + [
                      pl.BlockSpec(bias_tab.shape, lambda b, n, pt: (0, 0, 0)),
                      pl.BlockSpec(bias_own.shape, lambda b, n, pt: (0, 0))],
            out_specs=tok,
            scratch_shapes=[pltpu.VMEM((n_row, W_BR), BF16),
                            pltpu.VMEM((N_HEAD, D_HEAD, 128), F32), pltpu.VMEM((n_row, 128), F32),
                            pltpu.VMEM((n_row, 128), F32), pltpu.VMEM((n_blk, n_row, D_HEAD), F32)]),
        out_shape=jax.ShapeDtypeStruct((n_seq, t_len, W_BR), F32),
        compiler_params=_cparams(2),
    )(page_table, q, k_new, v_new, *([cache_k] * n_pg), *([cache_v] * n_pg), bias_tab, bias_own)


SOLVE_BASE = 4

def _unit_lower_inverse(a, size):
    n = a.shape[-1]
    ri = lax.broadcasted_iota(jnp.int32, (n, n), 0)
    ci = lax.broadcasted_iota(jnp.int32, (n, n), 1)
    eye = (ri == ci).astype(F32)

    def same_block(s):
        return (ri // s) == (ci // s)

    d = jnp.where(same_block(SOLVE_BASE), a, 0.0)
    inv = _dot(eye - d, eye + _dot(d, d))
    s = SOLVE_BASE
    while s < size:
        off = jnp.where(same_block(2 * s) & jnp.logical_not(same_block(s)), a, 0.0)
        inv = inv - _dot(_dot(inv, off), inv)
        s *= 2
    return inv


def _gdn_prompt_kernel(q_ref, k_ref, v_ref, beta_ref, gcol_ref, grow_ref, o_ref, s_ref, state_ref,
                       *, chunks):
    @pl.when(pl.program_id(1) == 0)
    def _():
        state_ref[...] = jnp.zeros(state_ref.shape, F32)

    ri = lax.broadcasted_iota(jnp.int32, (DN_CHUNK, DN_CHUNK), 0)
    ci = lax.broadcasted_iota(jnp.int32, (DN_CHUNK, DN_CHUNK), 1)
    tril = ri >= ci
    strict = ri > ci
    lower = tril.astype(F32)

    def split(x, r0):
        return [x[r0:r0 + DN_CHUNK, h * D_HEAD:(h + 1) * D_HEAD] for h in range(N_HEAD)]

    qa = q_ref[0] * (D_HEAD ** -0.5)
    ka, va, beta, gcol = k_ref[0], v_ref[0], beta_ref[0], gcol_ref[0]
    q, k, v, kt, gc, gr, b = [], [], [], [], [], [], []
    for c in range(chunks):
        r0 = c * DN_CHUNK
        q += split(qa, r0)
        k += split(ka, r0)
        v += split(va, r0)
        kt.append(ka[r0:r0 + DN_CHUNK].T.reshape(N_HEAD, D_HEAD, DN_CHUNK))
        gcum_col = jnp.dot(lower, gcol[r0:r0 + DN_CHUNK], precision=HIGHEST,
                           preferred_element_type=F32)
        gcum_row = lax.dot_general(grow_ref[c], lower, _NT, precision=HIGHEST,
                                   preferred_element_type=F32)
        gc += [gcum_col[:, h:h + 1] for h in range(N_HEAD)]
        gr += [gcum_row[h:h + 1, :] for h in range(N_HEAD)]
        b += [beta[r0:r0 + DN_CHUNK, h:h + 1] for h in range(N_HEAD)]
    q, k, v = jnp.stack(q, axis=0), jnp.stack(k, axis=0), jnp.stack(v, axis=0)
    kt = jnp.concatenate(kt, axis=0)
    gc, gr, b = jnp.stack(gc, axis=0), jnp.stack(gr, axis=0), jnp.stack(b, axis=0)
    decay = jnp.where(tril, jnp.exp(jnp.where(tril, gc - gr, 0.0)), 0.0)
    eg = jnp.exp(gc)
    glast = gr[:, :, DN_CHUNK - 1:DN_CHUNK]
    kb = k * b
    a = jnp.where(strict, _dot(kb, kt) * decay, 0.0)
    intra = _dot(q, kt) * decay
    x = _dot(_unit_lower_inverse(a, DN_CHUNK), jnp.concatenate([v * b, kb * eg], axis=2))
    u, w = x[:, :, :D_HEAD], x[:, :, D_HEAD:]
    q_dec = q * eg
    kt_dec = kt * jnp.exp(glast - gr)
    s_decay = jnp.exp(glast)

    s = state_ref[...]
    rows = []
    for c in range(chunks):
        sl = slice(c * N_HEAD, (c + 1) * N_HEAD)
        v_new = u[sl] - _dot(w[sl], s)
        o = _dot(q_dec[sl], s) + _dot(intra[sl], v_new)
        s = s * s_decay[sl] + _dot(kt_dec[sl], v_new)
        rows.append(jnp.concatenate([o[h] for h in range(N_HEAD)], axis=1))
    o_ref[0] = jnp.concatenate(rows, axis=0)
    state_ref[...] = s
    s_ref[0] = s


def _gdn_prompt_call(qc, kc, vc, beta, gcol, grow, n_seq):
    n_tok = qc.shape[0]
    t_len = n_tok // n_seq
    rows = TM
    chunks = rows // DN_CHUNK
    n_tiles = t_len // rows

    def tok(width):
        return pl.BlockSpec((1, rows, width), lambda b, t: (b, t, 0))

    o, s = pl.pallas_call(
        functools.partial(_gdn_prompt_kernel, chunks=chunks),
        grid=(n_seq, n_tiles),
        in_specs=[tok(W_BR), tok(W_BR), tok(W_BR), tok(N_HEAD), tok(N_HEAD),
                  pl.BlockSpec((chunks, N_HEAD, DN_CHUNK), lambda b, t: (b * n_tiles + t, 0, 0))],
        out_specs=[tok(W_BR), pl.BlockSpec((1, N_HEAD, D_HEAD, D_HEAD), lambda b, t: (b, 0, 0, 0))],
        out_shape=[jax.ShapeDtypeStruct((n_seq, t_len, W_BR), F32),
                   jax.ShapeDtypeStruct((n_seq, N_HEAD, D_HEAD, D_HEAD), F32)],
        scratch_shapes=[pltpu.VMEM((N_HEAD, D_HEAD, D_HEAD), F32)],
        compiler_params=_cparams(2),
    )(qc.reshape(n_seq, t_len, W_BR), kc.reshape(n_seq, t_len, W_BR), vc.reshape(n_seq, t_len, W_BR),
      beta.reshape(n_seq, t_len, N_HEAD), gcol.reshape(n_seq, t_len, N_HEAD), grow)
    return o.reshape(n_tok, W_BR), s


def _gdn_sample_kernel(q_ref, k_ref, v_ref, bcol_ref, gcol_ref, grow_ref, s0_ref, o_ref, s_ref,
                       *, t_len):
    n_row = N_HEAD * t_len
    ri = lax.broadcasted_iota(jnp.int32, (n_row, n_row), 0)
    ci = lax.broadcasted_iota(jnp.int32, (n_row, n_row), 1)
    same = (ri // t_len) == (ci // t_len)
    tril = same & (ri >= ci)
    strict = same & (ri > ci)
    q = _stack_heads(q_ref[0]) * (D_HEAD ** -0.5)
    k = _stack_heads(k_ref[0])
    v = _stack_heads(v_ref[0])
    b = bcol_ref[0]
    g_lane = jnp.broadcast_to(gcol_ref[0], (n_row, 128))
    gc = jnp.dot(tril.astype(F32), g_lane, precision=HIGHEST, preferred_element_type=F32)[:, 0:1]
    gl = jnp.dot(same.astype(F32), g_lane, precision=HIGHEST, preferred_element_type=F32)[:, 0:1]
    g_sub = jnp.broadcast_to(grow_ref[0], (SUBLANE, n_row))
    gr = lax.dot_general(g_sub, tril.astype(F32), _NT, precision=HIGHEST,
                         preferred_element_type=F32)[0:1, :]
    decay = jnp.where(tril, jnp.exp(jnp.where(tril, gc - gr, 0.0)), 0.0)
    eg = jnp.exp(gc)
    kb = k * b
    a = jnp.where(strict, lax.dot_general(kb, k, _NT, preferred_element_type=F32) * decay, 0.0)
    intra = lax.dot_general(q, k, _NT, preferred_element_type=F32) * decay
    vb = v * b
    vb_c = vb[:, 0:D_HEAD]
    for h in range(1, N_HEAD):
        vb_c = vb_c + vb[:, h * D_HEAD:(h + 1) * D_HEAD]
    x = _mm3(_unit_lower_inverse(a, t_len), jnp.concatenate([kb * eg, vb_c], axis=1))
    w, u = x[:, :W_BR], x[:, W_BR:]
    s0 = s0_ref[0]
    v_new = u - jnp.dot(w, s0, preferred_element_type=F32)
    o = jnp.dot(q * eg, s0, preferred_element_type=F32) + jnp.dot(intra, v_new, preferred_element_type=F32)
    er = lax.broadcasted_iota(jnp.int32, (W_BR, n_row), 0)
    ec = lax.broadcasted_iota(jnp.int32, (W_BR, n_row), 1)
    pick = ((er // D_HEAD) == (ec // t_len)) & ((ec % t_len) == t_len - 1)
    gl_state = jnp.dot(pick.astype(F32), jnp.broadcast_to(gc, (n_row, 128)), precision=HIGHEST,
                       preferred_element_type=F32)[:, 0:1]
    s_ref[0] = s0 * jnp.exp(gl_state) + lax.dot_general(
        k * jnp.exp(gl - gc), v_new, _TN, preferred_element_type=F32)
    o_ref[0] = _unstack_heads(jnp.concatenate([o] * N_HEAD, axis=1), t_len)


def _gdn_sample_call(qc, kc, vc, bcol, gcol, grow, s0):
    n_seq, t_len, _ = qc.shape
    n_row = N_HEAD * t_len
    assert t_len >= SOLVE_BASE and t_len & (t_len - 1) == 0
    tok = pl.BlockSpec((1, t_len, W_BR), lambda b: (b, 0, 0))
    colv = pl.BlockSpec((1, n_row, 1), lambda b: (b, 0, 0))
    st = pl.BlockSpec((1, W_BR, D_HEAD), lambda b: (b, 0, 0))
    return pl.pallas_call(
        functools.partial(_gdn_sample_kernel, t_len=t_len),
        grid=(n_seq,),
        in_specs=[tok, tok, tok, colv, colv, pl.BlockSpec((1, 1, n_row), lambda b: (b, 0, 0)), st],
        out_specs=[tok, st],
        out_shape=[jax.ShapeDtypeStruct((n_seq, t_len, W_BR), F32),
                   jax.ShapeDtypeStruct((n_seq, W_BR, D_HEAD), F32)],
        compiler_params=_cparams(1),
    )(qc, kc, vc, bcol, gcol, grow, s0)


def _out_kernel(x_ref, sc_ref, sh_ref, gate_ref, g1_ref, attn_ref, yb_ref, od_ref, og_ref, dng_ref,
                wmg_ref, woa_ref, wob_ref, woc_ref, wo_ref, gsum_ref, o_ref):
    x = x_ref[...]
    h = _mod_norm(x, g1_ref[...], sc_ref[0], sh_ref[0]).astype(BF16)
    mg = _sigmoid(jnp.dot(h, wmg_ref[...], preferred_element_type=F32))
    d = x.shape[1]
    ya = jnp.dot(attn_ref[...].astype(BF16), woa_ref[...], preferred_element_type=F32)
    yb = jnp.dot(yb_ref[...].astype(BF16), wob_ref[...], preferred_element_type=F32)
    od = od_ref[...]
    oc = (od * lax.rsqrt(_group_sumsq(od, gsum_ref[...]) * (1.0 / D_HEAD) + EPS) * dng_ref[...]) \
        * _silu(og_ref[...])
    yc = jnp.dot(oc.astype(BF16), woc_ref[...], preferred_element_type=F32)
    merged = mg[:, :d] * ya + mg[:, d:2 * d] * yb + mg[:, 2 * d:] * yc
    o_ref[...] = x + gate_ref[0] * jnp.dot(merged.astype(BF16), wo_ref[...], preferred_element_type=F32)


def _token_grid(n_tok, n_seq, prompt, d_model):
    if prompt:
        rows = TM
        n_tiles = n_tok // rows
        tiles_per_seq = n_tiles // n_seq
        mod_spec = pl.BlockSpec((1, 1, d_model), lambda t: (t // tiles_per_seq, 0, 0))
    else:
        rows, n_tiles, tiles_per_seq = n_tok, 1, 1
        mod_spec = pl.BlockSpec((1, rows, d_model), lambda t: (0, 0, 0))
    return rows, n_tiles, tiles_per_seq, mod_spec


def _out_call(x, sc, sh, gate, attn, yb, od, og, w, *, prompt, n_seq):
    n_tok, d_model = x.shape
    rows, n_tiles, _, mod_spec = _token_grid(n_tok, n_seq, prompt, d_model)

    def tok(width):
        return pl.BlockSpec((rows, width), lambda t: (t, 0))

    return pl.pallas_call(
        _out_kernel, grid=(n_tiles,),
        in_specs=[tok(d_model), mod_spec, mod_spec, mod_spec, _const_spec((1, d_model)),
                  tok(W_BR), tok(W_BR), tok(W_BR), tok(W_BR), _const_spec((1, W_BR)),
                  _const_spec(w['wmg'].shape), _const_spec(w['woa'].shape), _const_spec(w['wob'].shape),
                  _const_spec(w['woc'].shape), _const_spec(w['wo'].shape), _const_spec((W_BR, W_BR))],
        out_specs=tok(d_model),
        out_shape=jax.ShapeDtypeStruct((n_tok, d_model), F32),
        compiler_params=_cparams(1),
    )(x, sc, sh, gate, w['norm1_g'], attn, yb, od, og, w['dn_norm_g'], w['wmg'], w['woa'], w['wob'],
      w['woc'], w['wo'], w['gsum'])


def _ffn_kernel(x_ref, sc_ref, sh_ref, gate_ref, g2_ref, wg_ref, wu_ref, wd_ref, cw_ref, halo_ref,
                o_ref, tail_ref, xp_ref, *, rows, tiles_per_seq, stride, halo):
    first = (pl.program_id(0) % tiles_per_seq) == 0
    x = x_ref[...]
    h = _mod_norm(x, g2_ref[...], sc_ref[0], sh_ref[0]).astype(BF16)
    pre = jnp.dot(h, wg_ref[...], preferred_element_type=F32)
    hg, tail = _causal_conv(xp_ref, pre, cw_ref, halo_ref, first,
                            width=CONV_F, halo=halo, stride=stride, rows=rows)
    tail_ref[0] = tail
    act = _silu(hg) * jnp.dot(h, wu_ref[...], preferred_element_type=F32)
    o_ref[...] = x + gate_ref[0] * jnp.dot(act.astype(BF16), wd_ref[...], preferred_element_type=F32)


def _ffn_call(x, sc, sh, gate, w, halo_in, *, prompt, n_seq):
    n_tok, d_model = x.shape
    d_ff = w['wgate'].shape[1]
    rows, n_tiles, tiles_per_seq, mod_spec = _token_grid(n_tok, n_seq, prompt, d_model)
    stride = 1 if prompt else n_seq
    halo = SUBLANE if prompt else (CONV_F - 1) * n_seq
    n_grp = n_tiles // tiles_per_seq
    tok = pl.BlockSpec((rows, d_model), lambda t: (t, 0))
    per_seq = pl.BlockSpec((1, halo, d_ff), lambda t: (t // tiles_per_seq, 0, 0))
    kern = functools.partial(_ffn_kernel, rows=rows, tiles_per_seq=tiles_per_seq, stride=stride, halo=halo)
    return pl.pallas_call(
        kern, grid=(n_tiles,),
        in_specs=[tok, mod_spec, mod_spec, mod_spec, _const_spec((1, d_model)),
                  _const_spec(w['wgate'].shape), _const_spec(w['wup'].shape), _const_spec(w['wdown'].shape),
                  _const_spec((CONV_F, d_ff)), per_seq],
        out_specs=[tok, per_seq],
        out_shape=[jax.ShapeDtypeStruct((n_tok, d_model), F32),
                   jax.ShapeDtypeStruct((n_grp, halo, d_ff), F32)],
        scratch_shapes=[pltpu.VMEM((halo + rows, d_ff), F32)],
        compiler_params=_cparams(1),
    )(x, sc, sh, gate, w['norm2_g'], w['wgate'], w['wup'], w['wdown'], w['ffn_conv_w'], halo_in)


def _t5_bucket(rel):
    n = jnp.maximum(rel, 0)
    max_exact = REL_BUCKETS // 2
    nf = jnp.maximum(n, 1).astype(F32)
    large = max_exact + (jnp.log(nf / max_exact) / math.log(REL_MAX_DIST / max_exact)
                         * (REL_BUCKETS - max_exact)).astype(jnp.int32)
    large = jnp.minimum(large, REL_BUCKETS - 1)
    return jnp.where(n < max_exact, n, large)


def _bias_tables(rel_bias, n_blk_prompt, past_len, t_s):
    blk = MOBA_BLOCK

    def lookup(rel):
        onehot = jax.nn.one_hot(_t5_bucket(rel).reshape(-1), REL_BUCKETS, dtype=F32)
        tab = lax.dot_general(rel_bias, onehot, (((0,), (1,)), ((), ())), precision=HIGHEST)
        return jnp.where(rel.reshape(-1) >= 0, tab, NEG).reshape((N_HEAD,) + rel.shape)

    n_dist = min(n_blk_prompt, -(-(REL_MAX_DIST + blk - 1) // blk) + 1)
    d = jnp.arange(n_dist, dtype=jnp.int32)[:, None, None]
    ki = jnp.arange(blk, dtype=jnp.int32)[None, :, None]
    qj = jnp.arange(blk, dtype=jnp.int32)[None, None, :]
    tab_p = lookup(d * blk + qj - ki)
    n_blk_s = past_len // blk
    nb = jnp.arange(n_blk_s, dtype=jnp.int32)[:, None, None]
    tq = jnp.arange(t_s, dtype=jnp.int32)[None, :, None]
    ko = jnp.arange(blk, dtype=jnp.int32)[None, None, :]
    tab_s = lookup(past_len + tq - nb * blk - ko)
    tab_s = jnp.transpose(tab_s, (1, 0, 2, 3)).reshape(n_blk_s, N_HEAD * t_s, blk)
    rel_o = jnp.arange(t_s, dtype=jnp.int32)[:, None] - jnp.arange(128, dtype=jnp.int32)[None, :]
    tab_o = lookup(rel_o).reshape(N_HEAD * t_s, 128)
    return tab_p.astype(F32), tab_s.astype(F32), tab_o.astype(F32)


def _layer_weights(l, p, gsum):
    w_in = p['w_in'][l]
    cuts = [0, 3 * W_BR, 6 * W_BR, 10 * W_BR, 10 * W_BR + 2 * N_HEAD]
    wba = w_in[:, cuts[3]:cuts[4]].astype(BF16)
    row = lambda a: a.reshape(1, -1)
    return {
        'wa': w_in[:, cuts[0]:cuts[1]].astype(BF16),
        'wb': w_in[:, cuts[1]:cuts[2]].astype(BF16),
        'wc': w_in[:, cuts[2]:cuts[3]].astype(BF16),
        'wba': wba, 'wbat': wba.T,
        'wmg': w_in[:, cuts[4]:].astype(BF16),
        'norm1_g': row(p['norm1_g'][l]), 'norm2_g': row(p['norm2_g'][l]),
        'qn_g': row(jnp.tile(p['qn_g'][l], N_HEAD)), 'kn_g': row(jnp.tile(p['kn_g'][l], N_HEAD)),
        'dn_norm_g': row(jnp.tile(p['dn_norm_g'][l], N_HEAD)),
        'conv_b_w': p['conv_b_w'][l], 'dn_conv_w': p['dn_conv_w'][l], 'ffn_conv_w': p['ffn_conv_w'][l],
        'a_log_row': row(p['dn_a_log'][l]), 'dt_bias_row': row(p['dn_dt_bias'][l]),
        'a_log_col': p['dn_a_log'][l].reshape(-1, 1), 'dt_bias_col': p['dn_dt_bias'][l].reshape(-1, 1),
        'woa': p['w_oa'][l].astype(BF16), 'wob': p['w_ob'][l].astype(BF16), 'woc': p['w_oc'][l].astype(BF16),
        'wo': p['w_o'][l].astype(BF16),
        'wgate': p['w_gate'][l].astype(BF16), 'wup': p['w_up'][l].astype(BF16),
        'wdown': p['w_down'][l].astype(BF16),
        'gsum': gsum,
    }


def kernel(x_prompt, x_sample, cache_k, cache_v, state_conv_b, state_conv_dn, state_dn, state_conv_ffn,
           page_table, c_prompt, c_sample, rel_bias, w_ada, b_ada, norm1_g, norm2_g, w_in, qn_g, kn_g,
           conv_b_w, dn_conv_w, dn_a_log, dn_dt_bias, dn_norm_g, w_oa, w_ob, w_oc, w_o, w_gate, w_up,
           ffn_conv_w, w_down):
    bp, tp, d_model = x_prompt.shape
    bs, ts, _ = x_sample.shape
    depth = w_in.shape[0]
    d_ff = w_gate.shape[2]
    n_pool, page = cache_k.shape[1], cache_k.shape[2]
    past_len = page_table.shape[1] * page
    assert tp % MOBA_BLOCK == 0 and past_len % MOBA_BLOCK == 0 and MOBA_BLOCK == 2 * page
    assert ts <= 128 and past_len // MOBA_BLOCK >= MOBA_TOPK and (bs * ts) % SUBLANE == 0
    p = dict(w_in=w_in, norm1_g=norm1_g, norm2_g=norm2_g, qn_g=qn_g, kn_g=kn_g, conv_b_w=conv_b_w,
             dn_conv_w=dn_conv_w, dn_a_log=dn_a_log, dn_dt_bias=dn_dt_bias, dn_norm_g=dn_norm_g,
             w_oa=w_oa, w_ob=w_ob, w_oc=w_oc, w_o=w_o, w_gate=w_gate, w_up=w_up,
             ffn_conv_w=ffn_conv_w, w_down=w_down)

    lane = jnp.arange(W_BR) // D_HEAD
    gsum = (lane[:, None] == lane[None, :]).astype(BF16)
    tab_p, tab_s, tab_o = _bias_tables(rel_bias, tp // MOBA_BLOCK, past_len, ts)

    n_c = bp + bs
    c_rows = -(-n_c // SUBLANE) * SUBLANE
    c_all = jnp.concatenate([c_prompt, c_sample, jnp.zeros((c_rows - n_c, d_model), F32)], axis=0)
    mods = _ada_call(c_all, w_ada, b_ada)

    ck = jnp.transpose(cache_k, (0, 1, 3, 4, 2))
    cv = jnp.transpose(cache_v, (0, 1, 3, 4, 2))

    xp = x_prompt.reshape(bp * tp, d_model)
    xs = jnp.transpose(x_sample, (1, 0, 2)).reshape(ts * bs, d_model)

    def to_tm(a):
        return jnp.transpose(a, (1, 0, 2)).reshape(1, a.shape[1] * bs, a.shape[2])

    def from_tm(a, r):
        return jnp.transpose(a.reshape(r, bs, a.shape[-1]), (1, 0, 2))

    zero_b = jnp.zeros((bp, SUBLANE, W_BR), F32)
    zero_c = jnp.zeros((bp, SUBLANE, 3 * W_BR), F32)
    zero_f = jnp.zeros((bp, SUBLANE, d_ff), F32)

    outs = [[] for _ in range(12)]
    for l in range(depth):
        w = _layer_weights(l, p, gsum)
        mod_p = [m.reshape(bp, 1, d_model) for m in jnp.split(mods[l, :bp], 6, axis=-1)]
        mod_s = [jnp.tile(m, (ts, 1)).reshape(1, ts * bs, d_model)
                 for m in jnp.split(mods[l, bp:bp + bs], 6, axis=-1)]

        sh1, sc1, g1, sh2, sc2, g2 = mod_p
        (q, k, v, yb, tail_b, qc, kc, vc, og, beta, gcol, tail_c, k_bf, vt_bf, kmean, grow) = _in_call(
            xp, sc1, sh1, w, prompt=True, n_seq=bp, halo_b_in=zero_b, halo_c_in=zero_c)
        attn = _attn_prompt_call(q, k_bf, vt_bf, kmean, tab_p, bp)
        od, s_p = _gdn_prompt_call(qc, kc, vc, beta, gcol, grow, bp)
        xp = _out_call(xp, sc1, sh1, g1, attn, yb, od, og, w, prompt=True, n_seq=bp)
        xp, tail_f = _ffn_call(xp, sc2, sh2, g2, w, zero_f, prompt=True, n_seq=bp)
        outs[0].append(k.reshape(bp, tp, N_HEAD, D_HEAD))
        outs[1].append(v.reshape(bp, tp, N_HEAD, D_HEAD))
        outs[4].append(tail_b[:, SUBLANE - (CONV_B - 1):])
        outs[6].append(tail_c[:, SUBLANE - (CONV_C - 1):])
        outs[8].append(s_p)
        outs[10].append(tail_f[:, SUBLANE - (CONV_F - 1):])

        sh1, sc1, g1, sh2, sc2, g2 = mod_s
        (q, k, v, yb, tail_b, qc, kc, vc, og, beta, gcol, tail_c) = _in_call(
            xs, sc1, sh1, w, prompt=False, n_seq=bs,
            halo_b_in=to_tm(state_conv_b[l]), halo_c_in=to_tm(state_conv_dn[l]))
        q_b, k_b, v_b = from_tm(q, ts), from_tm(k, ts), from_tm(v, ts)
        attn = _attn_sample_call(page_table, q_b, k_b, v_b, ck, cv, l, tab_s, tab_o)
        beta_ht = jnp.transpose(beta.reshape(ts, bs, N_HEAD), (1, 2, 0)).reshape(bs, N_HEAD * ts)
        g_ht = jnp.transpose(gcol.reshape(ts, bs, N_HEAD), (1, 2, 0)).reshape(bs, N_HEAD * ts)
        od, s_s = _gdn_sample_call(from_tm(qc, ts), from_tm(kc, ts), from_tm(vc, ts),
                                   beta_ht[:, :, None], g_ht[:, :, None], g_ht[:, None, :],
                                   state_dn[l].reshape(bs, W_BR, D_HEAD))
        attn_tm = jnp.transpose(attn, (1, 0, 2)).reshape(ts * bs, W_BR)
        od_tm = jnp.transpose(od, (1, 0, 2)).reshape(ts * bs, W_BR)
        xs = _out_call(xs, sc1, sh1, g1, attn_tm, yb, od_tm, og, w, prompt=False, n_seq=bs)
        xs, tail_f = _ffn_call(xs, sc2, sh2, g2, w, to_tm(state_conv_ffn[l]), prompt=False, n_seq=bs)
        outs[2].append(k_b.reshape(bs, ts, N_HEAD, D_HEAD))
        outs[3].append(v_b.reshape(bs, ts, N_HEAD, D_HEAD))
        outs[5].append(from_tm(tail_b[0], CONV_B - 1))
        outs[7].append(from_tm(tail_c[0], CONV_C - 1))
        outs[9].append(s_s.reshape(bs, N_HEAD, D_HEAD, D_HEAD))
        outs[11].append(from_tm(tail_f[0], CONV_F - 1))

    y_p = xp.reshape(bp, tp, d_model)
    y_s = from_tm(xs, ts)
    return (y_p, y_s) + tuple(jnp.stack(o) for o in outs)
```

```python
import functools
import math

import jax
import jax.numpy as jnp
from jax import lax
from jax.experimental import pallas as pl
from jax.experimental.pallas import tpu as pltpu

F32 = jnp.float32
BF16 = jnp.bfloat16

N_HEAD = 8
D_HEAD = 64
W_BR = N_HEAD * D_HEAD
MOBA_BLOCK = 256
MOBA_TOPK = 3
CONV_B = 3
CONV_C = 4
CONV_F = 3
DN_CHUNK = 64
REL_BUCKETS = 32
REL_MAX_DIST = 2048
N_BRANCH = 3
EPS = 1e-6
NEG = -1e30

TM = MOBA_BLOCK
SUBLANE = 8
VMEM_LIMIT = 56 * 1024 * 1024
HIGHEST = lax.Precision.HIGHEST

_NT = (((1,), (1,)), ((), ()))
_TN = (((0,), (0,)), ((), ()))


def _const_spec(shape):
    nd = len(shape)
    return pl.BlockSpec(shape, lambda *_: (0,) * nd, pipeline_mode=pl.Buffered(1))


def _cparams(n_grid):
    return pltpu.CompilerParams(dimension_semantics=("arbitrary",) * n_grid,
                                vmem_limit_bytes=VMEM_LIMIT)


def _sigmoid(x):
    return 1.0 / (1.0 + jnp.exp(-x))


def _silu(x):
    return x * _sigmoid(x)


def _softplus(x):
    return jnp.maximum(x, 0.0) + jnp.log1p(jnp.exp(-jnp.abs(x)))


def _split_dot(a, b_bf16):
    hi = a.astype(BF16)
    lo = (a - hi.astype(F32)).astype(BF16)
    return (jnp.dot(hi, b_bf16, preferred_element_type=F32)
            + jnp.dot(lo, b_bf16, preferred_element_type=F32))


def _dot(a, b):
    if a.ndim == 3:
        return lax.dot_general(a, b, (((2,), (1,)), ((0,), (0,))), preferred_element_type=F32)
    return jnp.dot(a, b, preferred_element_type=F32)


def _mm3(a, b):
    ah = a.astype(BF16)
    al = (a - ah.astype(F32)).astype(BF16)
    bh = b.astype(BF16)
    bl = (b - bh.astype(F32)).astype(BF16)
    return _dot(ah, bh) + _dot(ah, bl) + _dot(al, bh)


def _group_sumsq(y, gsum_bf16):
    return _split_dot(y * y, gsum_bf16)


def _mod_norm(x, gain, sc, sh):
    ms = jnp.mean(x * x, axis=-1, keepdims=True)
    return (x * lax.rsqrt(ms + EPS) * gain) * (1.0 + sc) + sh


def _causal_conv(xp_ref, cur, w_ref, halo_ref, first, *, width, halo, stride, rows):
    @pl.when(first)
    def _():
        xp_ref[0:halo, :] = halo_ref[0]

    xp_ref[halo:halo + rows, :] = cur
    off = halo - (width - 1) * stride
    y = xp_ref[off:off + rows, :] * w_ref[0:1, :]
    for i in range(1, width):
        off = halo - (width - 1 - i) * stride
        y = y + xp_ref[off:off + rows, :] * w_ref[i:i + 1, :]
    tail = xp_ref[rows:rows + halo, :]
    xp_ref[0:halo, :] = tail
    return y, tail


def _ada_kernel(c_ref, w_ref, b_ref, o_ref):
    c = _silu(c_ref[...]).astype(BF16)
    o_ref[0] = jnp.dot(c, w_ref[0].astype(BF16), preferred_element_type=F32) + b_ref[0]


def _ada_call(c_all, w_ada, b_ada):
    depth, d_model, n_mod = w_ada.shape
    rows = c_all.shape[0]
    bn = 1536
    return pl.pallas_call(
        _ada_kernel,
        grid=(depth, n_mod // bn),
        in_specs=[pl.BlockSpec((rows, d_model), lambda l, j: (0, 0)),
                  pl.BlockSpec((1, d_model, bn), lambda l, j: (l, 0, j)),
                  pl.BlockSpec((1, 1, bn), lambda l, j: (l, 0, j))],
        out_specs=pl.BlockSpec((1, rows, bn), lambda l, j: (l, 0, j)),
        out_shape=jax.ShapeDtypeStruct((depth, rows, n_mod), F32),
        compiler_params=_cparams(2),
    )(c_all, w_ada, b_ada.reshape(depth, 1, n_mod))


def _in_kernel(*refs, prompt, rows, tiles_per_seq, stride, halo_b, halo_c):
    (x_ref, sc_ref, sh_ref, g1_ref, wa_ref, wb_ref, wc_ref, wba_ref, wbat_ref, qn_ref, kn_ref,
     cbw_ref, cdw_ref, alog_ref, dtb_ref, alogc_ref, dtbc_ref, gsum_ref, hb_ref, hc_ref) = refs[:20]
    outs = refs[20:-2]
    xpb_ref, xpc_ref = refs[-2:]
    if prompt:
        (q_ref, k_ref, v_ref, yb_ref, tailb_ref, qc_ref, kc_ref, vc_ref, og_ref, beta_ref,
         gcol_ref, tailc_ref, kbf_ref, vt_ref, kmean_ref, grow_ref) = outs
    else:
        (q_ref, k_ref, v_ref, yb_ref, tailb_ref, qc_ref, kc_ref, vc_ref, og_ref, beta_ref,
         gcol_ref, tailc_ref) = outs

    first = (pl.program_id(0) % tiles_per_seq) == 0
    gsum = gsum_ref[...]
    h = _mod_norm(x_ref[...], g1_ref[...], sc_ref[0], sh_ref[0])
    hb16 = h.astype(BF16)

    za = jnp.dot(hb16, wa_ref[...], preferred_element_type=F32)
    qa, ka, va = za[:, :W_BR], za[:, W_BR:2 * W_BR], za[:, 2 * W_BR:]
    q = (qa * lax.rsqrt(_group_sumsq(qa, gsum) * (1.0 / D_HEAD) + EPS) * qn_ref[...]) * (D_HEAD ** -0.5)
    k = ka * lax.rsqrt(_group_sumsq(ka, gsum) * (1.0 / D_HEAD) + EPS) * kn_ref[...]
    q_ref[...] = q
    k_ref[...] = k
    v_ref[...] = va
    if prompt:
        kbf_ref[0] = k.astype(BF16)
        vt_ref[0] = va.T.astype(BF16)
        kmean_ref[0] = jnp.sum(k, axis=0, keepdims=True) * (1.0 / MOBA_BLOCK)

    zb = jnp.dot(hb16, wb_ref[...], preferred_element_type=F32)
    hb, bg, cg = zb[:, :W_BR], zb[:, W_BR:2 * W_BR], zb[:, 2 * W_BR:]
    uc, tail_b = _causal_conv(xpb_ref, cg * hb, cbw_ref, hb_ref, first,
                              width=CONV_B, halo=halo_b, stride=stride, rows=rows)
    yb_ref[...] = bg * uc
    tailb_ref[0] = tail_b

    zc = jnp.dot(hb16, wc_ref[...], preferred_element_type=F32)
    qkv, tail_c = _causal_conv(xpc_ref, zc[:, :3 * W_BR], cdw_ref, hc_ref, first,
                               width=CONV_C, halo=halo_c, stride=stride, rows=rows)
    tailc_ref[0] = tail_c
    og_ref[...] = zc[:, 3 * W_BR:]
    qkv = _silu(qkv)
    qc, kc, vc = qkv[:, :W_BR], qkv[:, W_BR:2 * W_BR], qkv[:, 2 * W_BR:]
    qc_ref[...] = qc * lax.rsqrt(_group_sumsq(qc, gsum) + EPS)
    kc_ref[...] = kc * lax.rsqrt(_group_sumsq(kc, gsum) + EPS)
    vc_ref[...] = vc
    zba = jnp.dot(hb16, wba_ref[...], preferred_element_type=F32)
    beta_ref[...] = _sigmoid(zba[:, :N_HEAD])
    gcol_ref[...] = -jnp.exp(alog_ref[...]) * _softplus(zba[:, N_HEAD:] + dtb_ref[...])
    if prompt:
        zbat = lax.dot_general(wbat_ref[...], hb16, _NT, preferred_element_type=F32)
        g_row = -jnp.exp(alogc_ref[...]) * _softplus(zbat[N_HEAD:, :] + dtbc_ref[...])
        for j in range(rows // DN_CHUNK):
            grow_ref[j] = g_row[:, j * DN_CHUNK:(j + 1) * DN_CHUNK]


def _in_call(x, sc, sh, w, *, prompt, n_seq, halo_b_in, halo_c_in):
    n_tok, d_model = x.shape
    if prompt:
        rows, stride = TM, 1
        n_tiles = n_tok // rows
        tiles_per_seq = n_tiles // n_seq
        halo_b = halo_c = SUBLANE
        mod_spec = pl.BlockSpec((1, 1, d_model), lambda t: (t // tiles_per_seq, 0, 0))
    else:
        rows, stride = n_tok, n_seq
        n_tiles, tiles_per_seq = 1, 1
        halo_b, halo_c = (CONV_B - 1) * n_seq, (CONV_C - 1) * n_seq
        mod_spec = pl.BlockSpec((1, rows, d_model), lambda t: (0, 0, 0))
    n_grp = n_tiles // tiles_per_seq

    def tok(width):
        return pl.BlockSpec((rows, width), lambda t: (t, 0))

    def per_seq(r, width):
        return pl.BlockSpec((1, r, width), lambda t: (t // tiles_per_seq, 0, 0))

    in_specs = [tok(d_model), mod_spec, mod_spec, _const_spec((1, d_model)),
                _const_spec(w['wa'].shape), _const_spec(w['wb'].shape), _const_spec(w['wc'].shape),
                _const_spec(w['wba'].shape), _const_spec(w['wbat'].shape),
                _const_spec((1, W_BR)), _const_spec((1, W_BR)),
                _const_spec((CONV_B, W_BR)), _const_spec((CONV_C, 3 * W_BR)),
                _const_spec((1, N_HEAD)), _const_spec((1, N_HEAD)),
                _const_spec((N_HEAD, 1)), _const_spec((N_HEAD, 1)),
                _const_spec((W_BR, W_BR)),
                per_seq(halo_b, W_BR), per_seq(halo_c, 3 * W_BR)]
    out_specs = [tok(W_BR), tok(W_BR), tok(W_BR), tok(W_BR), per_seq(halo_b, W_BR),
                 tok(W_BR), tok(W_BR), tok(W_BR), tok(W_BR), tok(N_HEAD), tok(N_HEAD),
                 per_seq(halo_c, 3 * W_BR)]
    out_shape = [jax.ShapeDtypeStruct((n_tok, W_BR), F32)] * 4 + [
        jax.ShapeDtypeStruct((n_grp, halo_b, W_BR), F32)] + [
        jax.ShapeDtypeStruct((n_tok, W_BR), F32)] * 4 + [
        jax.ShapeDtypeStruct((n_tok, N_HEAD), F32)] * 2 + [
        jax.ShapeDtypeStruct((n_grp, halo_c, 3 * W_BR), F32)]
    if prompt:
        cpt = rows // DN_CHUNK
        out_specs += [pl.BlockSpec((1, rows, W_BR), lambda t: (t, 0, 0)),
                      pl.BlockSpec((1, W_BR, rows), lambda t: (t, 0, 0)),
                      pl.BlockSpec((1, 1, W_BR), lambda t: (t, 0, 0)),
                      pl.BlockSpec((cpt, N_HEAD, DN_CHUNK), lambda t: (t, 0, 0))]
        out_shape += [jax.ShapeDtypeStruct((n_tiles, rows, W_BR), BF16),
                      jax.ShapeDtypeStruct((n_tiles, W_BR, rows), BF16),
                      jax.ShapeDtypeStruct((n_tiles, 1, W_BR), F32),
                      jax.ShapeDtypeStruct((n_tiles * cpt, N_HEAD, DN_CHUNK), F32)]
    kern = functools.partial(_in_kernel, prompt=prompt, rows=rows, tiles_per_seq=tiles_per_seq,
                             stride=stride, halo_b=halo_b, halo_c=halo_c)
    return pl.pallas_call(
        kern, grid=(n_tiles,), in_specs=in_specs, out_specs=out_specs, out_shape=out_shape,
        scratch_shapes=[pltpu.VMEM((halo_b + rows, W_BR), F32),
                        pltpu.VMEM((halo_c + rows, 3 * W_BR), F32)],
        compiler_params=_cparams(1),
    )(x, sc, sh, w['norm1_g'], w['wa'], w['wb'], w['wc'], w['wba'], w['wbat'], w['qn_g'], w['kn_g'],
      w['conv_b_w'], w['dn_conv_w'], w['a_log_row'], w['dt_bias_row'], w['a_log_col'],
      w['dt_bias_col'], w['gsum'], halo_b_in, halo_c_in)


def _gate_scores(a, b):
    return lax.dot_general(a.astype(BF16), b.astype(BF16), _NT, preferred_element_type=F32)


def _top3_mask_cols(gates, own, n_blk):
    blk = lax.broadcasted_iota(jnp.int32, gates.shape, 0)
    g = jnp.where(blk < own, gates, NEG)
    mask = jnp.full(gates.shape, NEG, F32)
    for _ in range(MOBA_TOPK):
        mx = jnp.max(g, axis=0, keepdims=True)
        idx = jnp.min(jnp.where(g == mx, blk, n_blk), axis=0, keepdims=True)
        hit = blk == idx
        mask = jnp.where(hit & (idx < own), 0.0, mask)
        g = jnp.where(hit, -jnp.inf, g)
    return mask


def _attn_prompt_kernel(q_ref, k_ref, vt_ref, kmean_ref, bias_ref, o_ref, sel_ref, s_ref, p_ref,
                        *, n_blk, n_dist):
    qt = pl.program_id(2)
    qf = q_ref[0]
    lane = lax.broadcasted_iota(jnp.int32, qf.shape, 1)
    ones = jnp.ones((16, MOBA_BLOCK), BF16)
    qbs = []
    for hh in range(2):
        qh = jnp.where(lane // D_HEAD == hh, qf, 0.0)
        sel_ref[hh] = _top3_mask_cols(_gate_scores(kmean_ref[0], qh), qt, n_blk)
        qbs.append(qh.astype(BF16))

    def lhs_v(hh, n):
        return jnp.concatenate([vt_ref[0, n, hh * D_HEAD:(hh + 1) * D_HEAD, :], ones], axis=0)

    def scores(hh, n):
        return lax.dot_general(k_ref[0, n], qbs[hh], _NT, preferred_element_type=F32)

    def masked_scores(hh, n):
        d = jnp.clip(qt - n, 0, n_dist - 1)
        return scores(hh, n) + bias_ref[hh, d] + sel_ref[hh, pl.ds(n, 1), :]

    carry = []
    for hh in range(2):
        s = scores(hh, qt) + bias_ref[hh, 0]
        m = jnp.max(s, axis=0, keepdims=True)
        p_ref[1, hh] = jnp.exp(s - m).astype(BF16)
        carry.append((m, jnp.ones_like(m), jnp.zeros((D_HEAD + 16, TM), F32)))
        s_ref[0, hh] = masked_scores(hh, 0)
    carry = tuple(carry)

    def pending_pv(hh, n, p_slot):
        v_blk = jnp.where(n == 0, qt, n - 1)
        return jnp.dot(lhs_v(hh, v_blk), p_ref[p_slot, hh], preferred_element_type=F32)

    def step(n, carry, slot):
        nxt = [masked_scores(hh, jnp.minimum(n + 1, n_blk - 1)) for hh in range(2)]
        pv = [pending_pv(hh, n, 1 - slot) for hh in range(2)]
        out = []
        for hh in range(2):
            m, alpha, acc = carry[hh]
            s = s_ref[slot, hh]
            m_new = jnp.maximum(m, jnp.max(s, axis=0, keepdims=True))
            p_ref[slot, hh] = jnp.exp(s - m_new).astype(BF16)
            out.append((m_new, jnp.exp(m - m_new), alpha * acc + pv[hh]))
        for hh in range(2):
            s_ref[1 - slot, hh] = nxt[hh]
        return tuple(out)

    n_pairs = (qt + 1) // 2
    carry = lax.fori_loop(0, n_pairs, lambda i, c: step(2 * i + 1, step(2 * i, c, 0), 1), carry)
    outs = []
    for hh in range(2):
        _, alpha, acc = carry[hh]
        acc = alpha * acc + pending_pv(hh, 2 * n_pairs, 1)
        outs.append(acc[:D_HEAD] / acc[D_HEAD:D_HEAD + 1])
    o_ref[0] = jnp.concatenate(outs, axis=0).T


def _attn_prompt_call(q, k_bf, vt_bf, kmean, bias_tab, n_seq):
    n_tok = q.shape[0]
    t_len = n_tok // n_seq
    n_blk = t_len // MOBA_BLOCK
    n_dist = bias_tab.shape[1]
    q3 = q.reshape(n_seq, t_len, W_BR)
    k4 = k_bf.reshape(n_seq, n_blk, MOBA_BLOCK, W_BR)
    vt4 = vt_bf.reshape(n_seq, n_blk, W_BR, MOBA_BLOCK)
    km3 = kmean.reshape(n_seq, n_blk, W_BR)
    kern = functools.partial(_attn_prompt_kernel, n_blk=n_blk, n_dist=n_dist)
    out = pl.pallas_call(
        kern,
        grid=(n_seq, N_HEAD // 2, n_blk),
        in_specs=[pl.BlockSpec((1, TM, 128), lambda b, hp, t: (b, t, hp)),
                  pl.BlockSpec((1, n_blk, MOBA_BLOCK, 128), lambda b, hp, t: (b, 0, 0, hp)),
                  pl.BlockSpec((1, n_blk, 128, MOBA_BLOCK), lambda b, hp, t: (b, 0, hp, 0)),
                  pl.BlockSpec((1, n_blk, 128), lambda b, hp, t: (b, 0, hp)),
                  pl.BlockSpec((2, n_dist, MOBA_BLOCK, MOBA_BLOCK), lambda b, hp, t: (hp, 0, 0, 0))],
        out_specs=pl.BlockSpec((1, TM, 128), lambda b, hp, t: (b, t, hp)),
        out_shape=jax.ShapeDtypeStruct((n_seq, t_len, W_BR), F32),
        scratch_shapes=[pltpu.VMEM((2, n_blk, TM), F32),
                        pltpu.VMEM((2, 2, MOBA_BLOCK, TM), F32),
                        pltpu.VMEM((2, 2, MOBA_BLOCK, TM), BF16)],
        compiler_params=_cparams(3),
    )(q3, k4, vt4, km3, bias_tab)
    return out.reshape(n_tok, W_BR)


def _stack_heads(x):
    t_len = x.shape[0]
    tiled = jnp.concatenate([x] * N_HEAD, axis=0)
    row = lax.broadcasted_iota(jnp.int32, tiled.shape, 0)
    lane = lax.broadcasted_iota(jnp.int32, tiled.shape, 1)
    return jnp.where(row // t_len == lane // D_HEAD, tiled, 0.0)


def _unstack_heads(y, t_len):
    row = lax.broadcasted_iota(jnp.int32, y.shape, 0)
    lane = lax.broadcasted_iota(jnp.int32, y.shape, 1)
    y = jnp.where(row // t_len == lane // D_HEAD, y, 0.0)
    out = y[0:t_len]
    for h in range(1, N_HEAD):
        out = out + y[h * t_len:(h + 1) * t_len]
    return out


_BQK = (((2,), (2,)), ((0,), (0,)))
_BPV = (((2,), (1,)), ((0,), (0,)))


def _split_heads(x):
    return jnp.stack([x[:, h * D_HEAD:(h + 1) * D_HEAD] for h in range(N_HEAD)], axis=0)


SAMPLE_BLOCKS_PER_STEP = 4
PAGES_PER_BLOCK = 2


def _attn_sample_kernel(pt_ref, q_ref, kn_ref, vn_ref, *refs, n_blk, t_len, page):
    del pt_ref
    n_pg = SAMPLE_BLOCKS_PER_STEP * PAGES_PER_BLOCK
    ck_refs, cv_refs = refs[:n_pg], refs[n_pg:2 * n_pg]
    bias_ref, bown_ref, o_ref, qbd_ref, kmean_ref, m_ref, l_ref, oacc_ref = refs[2 * n_pg:]
    step = pl.program_id(1)
    n_row = N_HEAD * t_len

    @pl.when(step == 0)
    def _():
        qbd_ref[...] = _stack_heads(q_ref[0]).astype(BF16)
        m_ref[...] = jnp.zeros(m_ref.shape, F32)
        l_ref[...] = jnp.zeros(l_ref.shape, F32)
        kmean_ref[...] = jnp.zeros(kmean_ref.shape, F32)

    def own_heads(wide):
        return jnp.concatenate([wide[h * t_len:(h + 1) * t_len, h * D_HEAD:(h + 1) * D_HEAD]
                                for h in range(N_HEAD)], axis=0)

    qbd = qbd_ref[...]
    col = lax.broadcasted_iota(jnp.int32, m_ref.shape, 1)
    kcol = lax.broadcasted_iota(jnp.int32, kmean_ref.shape, 2)
    m_all, l_all, kmean_all = m_ref[...], l_ref[...], kmean_ref[...]
    for j in range(SAMPLE_BLOCKS_PER_STEP):
        n = step * SAMPLE_BLOCKS_PER_STEP + j
        s_pg, ksum = [], None
        for i in range(PAGES_PER_BLOCK):
            kt = ck_refs[PAGES_PER_BLOCK * j + i][0, 0]
            part = jnp.sum(kt, axis=2, keepdims=True)
            ksum = part if ksum is None else ksum + part
            s_pg.append(jnp.dot(qbd, kt.reshape(W_BR, page).astype(BF16), preferred_element_type=F32))
        kmean_all = jnp.where(kcol == n, ksum * (1.0 / MOBA_BLOCK), kmean_all)
        s = jnp.concatenate(s_pg, axis=1) + bias_ref[n]
        m = jnp.max(s, axis=1, keepdims=True)
        p = jnp.exp(s - m)
        l = jnp.sum(p, axis=1, keepdims=True)
        pb = p.astype(BF16)
        pv = None
        for i in range(PAGES_PER_BLOCK):
            vt = cv_refs[PAGES_PER_BLOCK * j + i][0, 0].reshape(W_BR, page).astype(BF16)
            part = lax.dot_general(pb[:, i * page:(i + 1) * page], vt, _NT,
                                   preferred_element_type=F32)
            pv = part if pv is None else pv + part
        oacc_ref[n] = own_heads(pv)
        m_all = jnp.where(col == n, m, m_all)
        l_all = jnp.where(col == n, l, l_all)
    m_ref[...] = m_all
    l_ref[...] = l_all
    kmean_ref[...] = kmean_all

    @pl.when(step == pl.num_programs(1) - 1)
    def _():
        gates = jnp.dot(qbd, kmean_all.reshape(W_BR, 128).astype(BF16),
                        preferred_element_type=F32)[:, 0:n_blk]
        blk = lax.broadcasted_iota(jnp.int32, gates.shape, 1)
        g = gates
        sel = jnp.zeros(gates.shape, jnp.bool_)
        for _ in range(MOBA_TOPK):
            mx = jnp.max(g, axis=1, keepdims=True)
            idx = jnp.min(jnp.where(g == mx, blk, n_blk), axis=1, keepdims=True)
            hit = blk == idx
            sel = sel | hit
            g = jnp.where(hit, -jnp.inf, g)
        m_all = m_ref[:, 0:n_blk]
        l_all = l_ref[:, 0:n_blk]

        pad = jnp.zeros((128 - t_len, W_BR), F32)
        kown = jnp.concatenate([kn_ref[0], pad], axis=0).astype(BF16)
        vown = jnp.concatenate([vn_ref[0], pad], axis=0).astype(BF16)
        s_own = lax.dot_general(qbd, kown, _NT, preferred_element_type=F32) + bown_ref[...]
        m_own = jnp.max(s_own, axis=1, keepdims=True)
        m_tot = jnp.maximum(m_own, jnp.max(jnp.where(sel, m_all, -jnp.inf), axis=1, keepdims=True))
        p_own = jnp.exp(s_own - m_tot)
        wgt = jnp.where(sel, jnp.exp(m_all - m_tot), 0.0)
        denom = jnp.sum(p_own, axis=1, keepdims=True) + jnp.sum(wgt * l_all, axis=1, keepdims=True)
        acc = own_heads(jnp.dot(p_own.astype(BF16), vown, preferred_element_type=F32))
        for i in range(n_blk):
            acc = acc + wgt[:, i:i + 1] * oacc_ref[i]
        out = acc / denom
        o_ref[0] = jnp.concatenate([out[h * t_len:(h + 1) * t_len] for h in range(N_HEAD)], axis=1)


def _attn_sample_call(page_table, q, k_new, v_new, cache_k, cache_v, layer, bias_tab, bias_own):
    n_seq, t_len, _ = q.shape
    n_pages = page_table.shape[1]
    page = cache_k.shape[4]
    n_blk = n_pages * page // MOBA_BLOCK
    n_row = N_HEAD * t_len
    tok = pl.BlockSpec((1, t_len, W_BR), lambda b, n, pt: (b, 0, 0))

    n_pg = SAMPLE_BLOCKS_PER_STEP * PAGES_PER_BLOCK
    assert page * PAGES_PER_BLOCK == MOBA_BLOCK and n_blk % SAMPLE_BLOCKS_PER_STEP == 0 and n_blk <= 128

    def page_spec(j):
        return pl.BlockSpec((1, 1, N_HEAD, D_HEAD, page),
                            lambda b, n, pt: (layer, pt[b, n_pg * n + j], 0, 0, 0))

    pages = [page_spec(j) for j in range(n_pg)]
    kern = functools.partial(_attn_sample_kernel, n_blk=n_blk, t_len=t_len, page=page)
    return pl.pallas_call(
        kern,
        grid_spec=pltpu.PrefetchScalarGridSpec(
            num_scalar_prefetch=1,
            grid=(n_seq, n_blk // SAMPLE_BLOCKS_PER_STEP),
            in_specs=[tok, tok, tok] + pages + pages + [
                      pl.BlockSpec(bias_tab.shape, lambda b, n, pt: (0, 0, 0)),
                      pl.BlockSpec(bias_own.shape, lambda b, n, pt: (0, 0))],
            out_specs=tok,
            scratch_shapes=[pltpu.VMEM((n_row, W_BR), BF16),
                            pltpu.VMEM((N_HEAD, D_HEAD, 128), F32), pltpu.VMEM((n_row, 128), F32),
                            pltpu.VMEM((n_row, 128), F32), pltpu.VMEM((n_blk, n_row, D_HEAD), F32)]),
        out_shape=jax.ShapeDtypeStruct((n_seq, t_len, W_BR), F32),
        compiler_params=_cparams(2),
    )(page_table, q, k_new, v_new, *([cache_k] * n_pg), *([cache_v] * n_pg), bias_tab, bias_own)


SOLVE_BASE = 4

def _unit_lower_inverse(a, size):
    n = a.shape[-1]
    ri = lax.broadcasted_iota(jnp.int32, (n, n), 0)
    ci = lax.broadcasted_iota(jnp.int32, (n, n), 1)
    eye = (ri == ci).astype(F32)

    def same_block(s):
        return (ri // s) == (ci // s)

    d = jnp.where(same_block(SOLVE_BASE), a, 0.0)
    inv = _dot(eye - d, eye + _dot(d, d))
    s = SOLVE_BASE
    while s < size:
        off = jnp.where(same_block(2 * s) & jnp.logical_not(same_block(s)), a, 0.0)
        inv = inv - _dot(_dot(inv, off), inv)
        s *= 2
    return inv


def _gdn_prompt_kernel(q_ref, k_ref, v_ref, beta_ref, gcol_ref, grow_ref, o_ref, s_ref, state_ref,
                       *, chunks):
    @pl.when(pl.program_id(1) == 0)
    def _():
        state_ref[...] = jnp.zeros(state_ref.shape, F32)

    ri = lax.broadcasted_iota(jnp.int32, (DN_CHUNK, DN_CHUNK), 0)
    ci = lax.broadcasted_iota(jnp.int32, (DN_CHUNK, DN_CHUNK), 1)
    tril = ri >= ci
    strict = ri > ci
    lower = tril.astype(F32)

    def split(x, r0):
        return [x[r0:r0 + DN_CHUNK, h * D_HEAD:(h + 1) * D_HEAD] for h in range(N_HEAD)]

    qa = q_ref[0] * (D_HEAD ** -0.5)
    ka, va, beta, gcol = k_ref[0], v_ref[0], beta_ref[0], gcol_ref[0]
    q, k, v, kt, gc, gr, b = [], [], [], [], [], [], []
    for c in range(chunks):
        r0 = c * DN_CHUNK
        q += split(qa, r0)
        k += split(ka, r0)
        v += split(va, r0)
        kt.append(ka[r0:r0 + DN_CHUNK].T.reshape(N_HEAD, D_HEAD, DN_CHUNK))
        gcum_col = jnp.dot(lower, gcol[r0:r0 + DN_CHUNK], precision=HIGHEST,
                           preferred_element_type=F32)
        gcum_row = lax.dot_general(grow_ref[c], lower, _NT, precision=HIGHEST,
                                   preferred_element_type=F32)
        gc += [gcum_col[:, h:h + 1] for h in range(N_HEAD)]
        gr += [gcum_row[h:h + 1, :] for h in range(N_HEAD)]
        b += [beta[r0:r0 + DN_CHUNK, h:h + 1] for h in range(N_HEAD)]
    q, k, v = jnp.stack(q, axis=0), jnp.stack(k, axis=0), jnp.stack(v, axis=0)
    kt = jnp.concatenate(kt, axis=0)
    gc, gr, b = jnp.stack(gc, axis=0), jnp.stack(gr, axis=0), jnp.stack(b, axis=0)
    decay = jnp.where(tril, jnp.exp(jnp.where(tril, gc - gr, 0.0)), 0.0)
    eg = jnp.exp(gc)
    glast = gr[:, :, DN_CHUNK - 1:DN_CHUNK]
    kb = k * b
    a = jnp.where(strict, _dot(kb, kt) * decay, 0.0)
    intra = _dot(q, kt) * decay
    x = _dot(_unit_lower_inverse(a, DN_CHUNK), jnp.concatenate([v * b, kb * eg], axis=2))
    u, w = x[:, :, :D_HEAD], x[:, :, D_HEAD:]
    q_dec = q * eg
    kt_dec = kt * jnp.exp(glast - gr)
    s_decay = jnp.exp(glast)

    s = state_ref[...]
    rows = []
    for c in range(chunks):
        sl = slice(c * N_HEAD, (c + 1) * N_HEAD)
        v_new = u[sl] - _dot(w[sl], s)
        o = _dot(q_dec[sl], s) + _dot(intra[sl], v_new)
        s = s * s_decay[sl] + _dot(kt_dec[sl], v_new)
        rows.append(jnp.concatenate([o[h] for h in range(N_HEAD)], axis=1))
    o_ref[0] = jnp.concatenate(rows, axis=0)
    state_ref[...] = s
    s_ref[0] = s


def _gdn_prompt_call(qc, kc, vc, beta, gcol, grow, n_seq):
    n_tok = qc.shape[0]
    t_len = n_tok // n_seq
    rows = TM
    chunks = rows // DN_CHUNK
    n_tiles = t_len // rows

    def tok(width):
        return pl.BlockSpec((1, rows, width), lambda b, t: (b, t, 0))

    o, s = pl.pallas_call(
        functools.partial(_gdn_prompt_kernel, chunks=chunks),
        grid=(n_seq, n_tiles),
        in_specs=[tok(W_BR), tok(W_BR), tok(W_BR), tok(N_HEAD), tok(N_HEAD),
                  pl.BlockSpec((chunks, N_HEAD, DN_CHUNK), lambda b, t: (b * n_tiles + t, 0, 0))],
        out_specs=[tok(W_BR), pl.BlockSpec((1, N_HEAD, D_HEAD, D_HEAD), lambda b, t: (b, 0, 0, 0))],
        out_shape=[jax.ShapeDtypeStruct((n_seq, t_len, W_BR), F32),
                   jax.ShapeDtypeStruct((n_seq, N_HEAD, D_HEAD, D_HEAD), F32)],
        scratch_shapes=[pltpu.VMEM((N_HEAD, D_HEAD, D_HEAD), F32)],
        compiler_params=_cparams(2),
    )(qc.reshape(n_seq, t_len, W_BR), kc.reshape(n_seq, t_len, W_BR), vc.reshape(n_seq, t_len, W_BR),
      beta.reshape(n_seq, t_len, N_HEAD), gcol.reshape(n_seq, t_len, N_HEAD), grow)
    return o.reshape(n_tok, W_BR), s


def _gdn_sample_kernel(q_ref, k_ref, v_ref, bcol_ref, gcol_ref, grow_ref, s0_ref, o_ref, s_ref,
                       *, t_len):
    n_row = N_HEAD * t_len
    ri = lax.broadcasted_iota(jnp.int32, (n_row, n_row), 0)
    ci = lax.broadcasted_iota(jnp.int32, (n_row, n_row), 1)
    same = (ri // t_len) == (ci // t_len)
    tril = same & (ri >= ci)
    strict = same & (ri > ci)
    q = _stack_heads(q_ref[0]) * (D_HEAD ** -0.5)
    k = _stack_heads(k_ref[0])
    v = _stack_heads(v_ref[0])
    b = bcol_ref[0]
    g_lane = jnp.broadcast_to(gcol_ref[0], (n_row, 128))
    gc = jnp.dot(tril.astype(F32), g_lane, precision=HIGHEST, preferred_element_type=F32)[:, 0:1]
    gl = jnp.dot(same.astype(F32), g_lane, precision=HIGHEST, preferred_element_type=F32)[:, 0:1]
    g_sub = jnp.broadcast_to(grow_ref[0], (SUBLANE, n_row))
    gr = lax.dot_general(g_sub, tril.astype(F32), _NT, precision=HIGHEST,
                         preferred_element_type=F32)[0:1, :]
    decay = jnp.where(tril, jnp.exp(jnp.where(tril, gc - gr, 0.0)), 0.0)
    eg = jnp.exp(gc)
    kb = k * b
    a = jnp.where(strict, lax.dot_general(kb, k, _NT, preferred_element_type=F32) * decay, 0.0)
    intra = lax.dot_general(q, k, _NT, preferred_element_type=F32) * decay
    vb = v * b
    vb_c = vb[:, 0:D_HEAD]
    for h in range(1, N_HEAD):
        vb_c = vb_c + vb[:, h * D_HEAD:(h + 1) * D_HEAD]
    x = _mm3(_unit_lower_inverse(a, t_len), jnp.concatenate([kb * eg, vb_c], axis=1))
    w, u = x[:, :W_BR], x[:, W_BR:]
    s0 = s0_ref[0]
    v_new = u - jnp.dot(w, s0, preferred_element_type=F32)
    o = jnp.dot(q * eg, s0, preferred_element_type=F32) + jnp.dot(intra, v_new, preferred_element_type=F32)
    er = lax.broadcasted_iota(jnp.int32, (W_BR, n_row), 0)
    ec = lax.broadcasted_iota(jnp.int32, (W_BR, n_row), 1)
    pick = ((er // D_HEAD) == (ec // t_len)) & ((ec % t_len) == t_len - 1)
    gl_state = jnp.dot(pick.astype(F32), jnp.broadcast_to(gc, (n_row, 128)), precision=HIGHEST,
                       preferred_element_type=F32)[:, 0:1]
    s_ref[0] = s0 * jnp.exp(gl_state) + lax.dot_general(
        k * jnp.exp(gl - gc), v_new, _TN, preferred_element_type=F32)
    o_ref[0] = _unstack_heads(jnp.concatenate([o] * N_HEAD, axis=1), t_len)


def _gdn_sample_call(qc, kc, vc, bcol, gcol, grow, s0):
    n_seq, t_len, _ = qc.shape
    n_row = N_HEAD * t_len
    assert t_len >= SOLVE_BASE and t_len & (t_len - 1) == 0
    tok = pl.BlockSpec((1, t_len, W_BR), lambda b: (b, 0, 0))
    colv = pl.BlockSpec((1, n_row, 1), lambda b: (b, 0, 0))
    st = pl.BlockSpec((1, W_BR, D_HEAD), lambda b: (b, 0, 0))
    return pl.pallas_call(
        functools.partial(_gdn_sample_kernel, t_len=t_len),
        grid=(n_seq,),
        in_specs=[tok, tok, tok, colv, colv, pl.BlockSpec((1, 1, n_row), lambda b: (b, 0, 0)), st],
        out_specs=[tok, st],
        out_shape=[jax.ShapeDtypeStruct((n_seq, t_len, W_BR), F32),
                   jax.ShapeDtypeStruct((n_seq, W_BR, D_HEAD), F32)],
        compiler_params=_cparams(1),
    )(qc, kc, vc, bcol, gcol, grow, s0)


def _out_kernel(x_ref, sc_ref, sh_ref, gate_ref, g1_ref, attn_ref, yb_ref, od_ref, og_ref, dng_ref,
                wmg_ref, woa_ref, wob_ref, woc_ref, wo_ref, gsum_ref, o_ref):
    x = x_ref[...]
    h = _mod_norm(x, g1_ref[...], sc_ref[0], sh_ref[0]).astype(BF16)
    mg = _sigmoid(jnp.dot(h, wmg_ref[...], preferred_element_type=F32))
    d = x.shape[1]
    ya = jnp.dot(attn_ref[...].astype(BF16), woa_ref[...], preferred_element_type=F32)
    yb = jnp.dot(yb_ref[...].astype(BF16), wob_ref[...], preferred_element_type=F32)
    od = od_ref[...]
    oc = (od * lax.rsqrt(_group_sumsq(od, gsum_ref[...]) * (1.0 / D_HEAD) + EPS) * dng_ref[...]) \
        * _silu(og_ref[...])
    yc = jnp.dot(oc.astype(BF16), woc_ref[...], preferred_element_type=F32)
    merged = mg[:, :d] * ya + mg[:, d:2 * d] * yb + mg[:, 2 * d:] * yc
    o_ref[...] = x + gate_ref[0] * jnp.dot(merged.astype(BF16), wo_ref[...], preferred_element_type=F32)


def _token_grid(n_tok, n_seq, prompt, d_model):
    if prompt:
        rows = TM
        n_tiles = n_tok // rows
        tiles_per_seq = n_tiles // n_seq
        mod_spec = pl.BlockSpec((1, 1, d_model), lambda t: (t // tiles_per_seq, 0, 0))
    else:
        rows, n_tiles, tiles_per_seq = n_tok, 1, 1
        mod_spec = pl.BlockSpec((1, rows, d_model), lambda t: (0, 0, 0))
    return rows, n_tiles, tiles_per_seq, mod_spec


def _out_call(x, sc, sh, gate, attn, yb, od, og, w, *, prompt, n_seq):
    n_tok, d_model = x.shape
    rows, n_tiles, _, mod_spec = _token_grid(n_tok, n_seq, prompt, d_model)

    def tok(width):
        return pl.BlockSpec((rows, width), lambda t: (t, 0))

    return pl.pallas_call(
        _out_kernel, grid=(n_tiles,),
        in_specs=[tok(d_model), mod_spec, mod_spec, mod_spec, _const_spec((1, d_model)),
                  tok(W_BR), tok(W_BR), tok(W_BR), tok(W_BR), _const_spec((1, W_BR)),
                  _const_spec(w['wmg'].shape), _const_spec(w['woa'].shape), _const_spec(w['wob'].shape),
                  _const_spec(w['woc'].shape), _const_spec(w['wo'].shape), _const_spec((W_BR, W_BR))],
        out_specs=tok(d_model),
        out_shape=jax.ShapeDtypeStruct((n_tok, d_model), F32),
        compiler_params=_cparams(1),
    )(x, sc, sh, gate, w['norm1_g'], attn, yb, od, og, w['dn_norm_g'], w['wmg'], w['woa'], w['wob'],
      w['woc'], w['wo'], w['gsum'])


def _ffn_kernel(x_ref, sc_ref, sh_ref, gate_ref, g2_ref, wg_ref, wu_ref, wd_ref, cw_ref, halo_ref,
                o_ref, tail_ref, xp_ref, *, rows, tiles_per_seq, stride, halo):
    first = (pl.program_id(0) % tiles_per_seq) == 0
    x = x_ref[...]
    h = _mod_norm(x, g2_ref[...], sc_ref[0], sh_ref[0]).astype(BF16)
    pre = jnp.dot(h, wg_ref[...], preferred_element_type=F32)
    hg, tail = _causal_conv(xp_ref, pre, cw_ref, halo_ref, first,
                            width=CONV_F, halo=halo, stride=stride, rows=rows)
    tail_ref[0] = tail
    act = _silu(hg) * jnp.dot(h, wu_ref[...], preferred_element_type=F32)
    o_ref[...] = x + gate_ref[0] * jnp.dot(act.astype(BF16), wd_ref[...], preferred_element_type=F32)


def _ffn_call(x, sc, sh, gate, w, halo_in, *, prompt, n_seq):
    n_tok, d_model = x.shape
    d_ff = w['wgate'].shape[1]
    rows, n_tiles, tiles_per_seq, mod_spec = _token_grid(n_tok, n_seq, prompt, d_model)
    stride = 1 if prompt else n_seq
    halo = SUBLANE if prompt else (CONV_F - 1) * n_seq
    n_grp = n_tiles // tiles_per_seq
    tok = pl.BlockSpec((rows, d_model), lambda t: (t, 0))
    per_seq = pl.BlockSpec((1, halo, d_ff), lambda t: (t // tiles_per_seq, 0, 0))
    kern = functools.partial(_ffn_kernel, rows=rows, tiles_per_seq=tiles_per_seq, stride=stride, halo=halo)
    return pl.pallas_call(
        kern, grid=(n_tiles,),
        in_specs=[tok, mod_spec, mod_spec, mod_spec, _const_spec((1, d_model)),
                  _const_spec(w['wgate'].shape), _const_spec(w['wup'].shape), _const_spec(w['wdown'].shape),
                  _const_spec((CONV_F, d_ff)), per_seq],
        out_specs=[tok, per_seq],
        out_shape=[jax.ShapeDtypeStruct((n_tok, d_model), F32),
                   jax.ShapeDtypeStruct((n_grp, halo, d_ff), F32)],
        scratch_shapes=[pltpu.VMEM((halo + rows, d_ff), F32)],
        compiler_params=_cparams(1),
    )(x, sc, sh, gate, w['norm2_g'], w['wgate'], w['wup'], w['wdown'], w['ffn_conv_w'], halo_in)


def _t5_bucket(rel):
    n = jnp.maximum(rel, 0)
    max_exact = REL_BUCKETS // 2
    nf = jnp.maximum(n, 1).astype(F32)
    large = max_exact + (jnp.log(nf / max_exact) / math.log(REL_MAX_DIST / max_exact)
                         * (REL_BUCKETS - max_exact)).astype(jnp.int32)
    large = jnp.minimum(large, REL_BUCKETS - 1)
    return jnp.where(n < max_exact, n, large)


def _bias_tables(rel_bias, n_blk_prompt, past_len, t_s):
    blk = MOBA_BLOCK

    def lookup(rel):
        onehot = jax.nn.one_hot(_t5_bucket(rel).reshape(-1), REL_BUCKETS, dtype=F32)
        tab = lax.dot_general(rel_bias, onehot, (((0,), (1,)), ((), ())), precision=HIGHEST)
        return jnp.where(rel.reshape(-1) >= 0, tab, NEG).reshape((N_HEAD,) + rel.shape)

    n_dist = min(n_blk_prompt, -(-(REL_MAX_DIST + blk - 1) // blk) + 1)
    d = jnp.arange(n_dist, dtype=jnp.int32)[:, None, None]
    ki = jnp.arange(blk, dtype=jnp.int32)[None, :, None]
    qj = jnp.arange(blk, dtype=jnp.int32)[None, None, :]
    tab_p = lookup(d * blk + qj - ki)
    n_blk_s = past_len // blk
    nb = jnp.arange(n_blk_s, dtype=jnp.int32)[:, None, None]
    tq = jnp.arange(t_s, dtype=jnp.int32)[None, :, None]
    ko = jnp.arange(blk, dtype=jnp.int32)[None, None, :]
    tab_s = lookup(past_len + tq - nb * blk - ko)
    tab_s = jnp.transpose(tab_s, (1, 0, 2, 3)).reshape(n_blk_s, N_HEAD * t_s, blk)
    rel_o = jnp.arange(t_s, dtype=jnp.int32)[:, None] - jnp.arange(128, dtype=jnp.int32)[None, :]
    tab_o = lookup(rel_o).reshape(N_HEAD * t_s, 128)
    return tab_p.astype(F32), tab_s.astype(F32), tab_o.astype(F32)


def _layer_weights(l, p, gsum):
    w_in = p['w_in'][l]
    cuts = [0, 3 * W_BR, 6 * W_BR, 10 * W_BR, 10 * W_BR + 2 * N_HEAD]
    wba = w_in[:, cuts[3]:cuts[4]].astype(BF16)
    row = lambda a: a.reshape(1, -1)
    return {
        'wa': w_in[:, cuts[0]:cuts[1]].astype(BF16),
        'wb': w_in[:, cuts[1]:cuts[2]].astype(BF16),
        'wc': w_in[:, cuts[2]:cuts[3]].astype(BF16),
        'wba': wba, 'wbat': wba.T,
        'wmg': w_in[:, cuts[4]:].astype(BF16),
        'norm1_g': row(p['norm1_g'][l]), 'norm2_g': row(p['norm2_g'][l]),
        'qn_g': row(jnp.tile(p['qn_g'][l], N_HEAD)), 'kn_g': row(jnp.tile(p['kn_g'][l], N_HEAD)),
        'dn_norm_g': row(jnp.tile(p['dn_norm_g'][l], N_HEAD)),
        'conv_b_w': p['conv_b_w'][l], 'dn_conv_w': p['dn_conv_w'][l], 'ffn_conv_w': p['ffn_conv_w'][l],
        'a_log_row': row(p['dn_a_log'][l]), 'dt_bias_row': row(p['dn_dt_bias'][l]),
        'a_log_col': p['dn_a_log'][l].reshape(-1, 1), 'dt_bias_col': p['dn_dt_bias'][l].reshape(-1, 1),
        'woa': p['w_oa'][l].astype(BF16), 'wob': p['w_ob'][l].astype(BF16), 'woc': p['w_oc'][l].astype(BF16),
        'wo': p['w_o'][l].astype(BF16),
        'wgate': p['w_gate'][l].astype(BF16), 'wup': p['w_up'][l].astype(BF16),
        'wdown': p['w_down'][l].astype(BF16),
        'gsum': gsum,
    }


def kernel(x_prompt, x_sample, cache_k, cache_v, state_conv_b, state_conv_dn, state_dn, state_conv_ffn,
           page_table, c_prompt, c_sample, rel_bias, w_ada, b_ada, norm1_g, norm2_g, w_in, qn_g, kn_g,
           conv_b_w, dn_conv_w, dn_a_log, dn_dt_bias, dn_norm_g, w_oa, w_ob, w_oc, w_o, w_gate, w_up,
           ffn_conv_w, w_down):
    bp, tp, d_model = x_prompt.shape
    bs, ts, _ = x_sample.shape
    depth = w_in.shape[0]
    d_ff = w_gate.shape[2]
    n_pool, page = cache_k.shape[1], cache_k.shape[2]
    past_len = page_table.shape[1] * page
    assert tp % MOBA_BLOCK == 0 and past_len % MOBA_BLOCK == 0 and MOBA_BLOCK == 2 * page
    assert ts <= 128 and past_len // MOBA_BLOCK >= MOBA_TOPK and (bs * ts) % SUBLANE == 0
    p = dict(w_in=w_in, norm1_g=norm1_g, norm2_g=norm2_g, qn_g=qn_g, kn_g=kn_g, conv_b_w=conv_b_w,
             dn_conv_w=dn_conv_w, dn_a_log=dn_a_log, dn_dt_bias=dn_dt_bias, dn_norm_g=dn_norm_g,
             w_oa=w_oa, w_ob=w_ob, w_oc=w_oc, w_o=w_o, w_gate=w_gate, w_up=w_up,
             ffn_conv_w=ffn_conv_w, w_down=w_down)

    lane = jnp.arange(W_BR) // D_HEAD
    gsum = (lane[:, None] == lane[None, :]).astype(BF16)
    tab_p, tab_s, tab_o = _bias_tables(rel_bias, tp // MOBA_BLOCK, past_len, ts)

    n_c = bp + bs
    c_rows = -(-n_c // SUBLANE) * SUBLANE
    c_all = jnp.concatenate([c_prompt, c_sample, jnp.zeros((c_rows - n_c, d_model), F32)], axis=0)
    mods = _ada_call(c_all, w_ada, b_ada)

    ck = jnp.transpose(cache_k, (0, 1, 3, 4, 2))
    cv = jnp.transpose(cache_v, (0, 1, 3, 4, 2))

    xp = x_prompt.reshape(bp * tp, d_model)
    xs = jnp.transpose(x_sample, (1, 0, 2)).reshape(ts * bs, d_model)

    def to_tm(a):
        return jnp.transpose(a, (1, 0, 2)).reshape(1, a.shape[1] * bs, a.shape[2])

    def from_tm(a, r):
        return jnp.transpose(a.reshape(r, bs, a.shape[-1]), (1, 0, 2))

    zero_b = jnp.zeros((bp, SUBLANE, W_BR), F32)
    zero_c = jnp.zeros((bp, SUBLANE, 3 * W_BR), F32)
    zero_f = jnp.zeros((bp, SUBLANE, d_ff), F32)

    outs = [[] for _ in range(12)]
    for l in range(depth):
        w = _layer_weights(l, p, gsum)
        mod_p = [m.reshape(bp, 1, d_model) for m in jnp.split(mods[l, :bp], 6, axis=-1)]
        mod_s = [jnp.tile(m, (ts, 1)).reshape(1, ts * bs, d_model)
                 for m in jnp.split(mods[l, bp:bp + bs], 6, axis=-1)]

        sh1, sc1, g1, sh2, sc2, g2 = mod_p
        (q, k, v, yb, tail_b, qc, kc, vc, og, beta, gcol, tail_c, k_bf, vt_bf, kmean, grow) = _in_call(
            xp, sc1, sh1, w, prompt=True, n_seq=bp, halo_b_in=zero_b, halo_c_in=zero_c)
        attn = _attn_prompt_call(q, k_bf, vt_bf, kmean, tab_p, bp)
        od, s_p = _gdn_prompt_call(qc, kc, vc, beta, gcol, grow, bp)
        xp = _out_call(xp, sc1, sh1, g1, attn, yb, od, og, w, prompt=True, n_seq=bp)
        xp, tail_f = _ffn_call(xp, sc2, sh2, g2, w, zero_f, prompt=True, n_seq=bp)
        outs[0].append(k.reshape(bp, tp, N_HEAD, D_HEAD))
        outs[1].append(v.reshape(bp, tp, N_HEAD, D_HEAD))
        outs[4].append(tail_b[:, SUBLANE - (CONV_B - 1):])
        outs[6].append(tail_c[:, SUBLANE - (CONV_C - 1):])
        outs[8].append(s_p)
        outs[10].append(tail_f[:, SUBLANE - (CONV_F - 1):])

        sh1, sc1, g1, sh2, sc2, g2 = mod_s
        (q, k, v, yb, tail_b, qc, kc, vc, og, beta, gcol, tail_c) = _in_call(
            xs, sc1, sh1, w, prompt=False, n_seq=bs,
            halo_b_in=to_tm(state_conv_b[l]), halo_c_in=to_tm(state_conv_dn[l]))
        q_b, k_b, v_b = from_tm(q, ts), from_tm(k, ts), from_tm(v, ts)
        attn = _attn_sample_call(page_table, q_b, k_b, v_b, ck, cv, l, tab_s, tab_o)
        beta_ht = jnp.transpose(beta.reshape(ts, bs, N_HEAD), (1, 2, 0)).reshape(bs, N_HEAD * ts)
        g_ht = jnp.transpose(gcol.reshape(ts, bs, N_HEAD), (1, 2, 0)).reshape(bs, N_HEAD * ts)
        od, s_s = _gdn_sample_call(from_tm(qc, ts), from_tm(kc, ts), from_tm(vc, ts),
                                   beta_ht[:, :, None], g_ht[:, :, None], g_ht[:, None, :],
                                   state_dn[l].reshape(bs, W_BR, D_HEAD))
        attn_tm = jnp.transpose(attn, (1, 0, 2)).reshape(ts * bs, W_BR)
        od_tm = jnp.transpose(od, (1, 0, 2)).reshape(ts * bs, W_BR)
        xs = _out_call(xs, sc1, sh1, g1, attn_tm, yb, od_tm, og, w, prompt=False, n_seq=bs)
        xs, tail_f = _ffn_call(xs, sc2, sh2, g2, w, to_tm(state_conv_ffn[l]), prompt=False, n_seq=bs)
        outs[2].append(k_b.reshape(bs, ts, N_HEAD, D_HEAD))
        outs[3].append(v_b.reshape(bs, ts, N_HEAD, D_HEAD))
        outs[5].append(from_tm(tail_b[0], CONV_B - 1))
        outs[7].append(from_tm(tail_c[0], CONV_C - 1))
        outs[9].append(s_s.reshape(bs, N_HEAD, D_HEAD, D_HEAD))
        outs[11].append(from_tm(tail_f[0], CONV_F - 1))

    y_p = xp.reshape(bp, tp, d_model)
    y_s = from_tm(xs, ts)
    return (y_p, y_s) + tuple(jnp.stack(o) for o in outs)
```

```python
import functools
import math

import jax
import jax.numpy as jnp
from jax import lax
from jax.experimental import pallas as pl
from jax.experimental.pallas import tpu as pltpu

F32 = jnp.float32
BF16 = jnp.bfloat16

N_HEAD = 8
D_HEAD = 64
W_BR = N_HEAD * D_HEAD
MOBA_BLOCK = 256
MOBA_TOPK = 3
CONV_B = 3
CONV_C = 4
CONV_F = 3
DN_CHUNK = 64
REL_BUCKETS = 32
REL_MAX_DIST = 2048
N_BRANCH = 3
EPS = 1e-6
NEG = -1e30

TM = MOBA_BLOCK
SUBLANE = 8
VMEM_LIMIT = 56 * 1024 * 1024
HIGHEST = lax.Precision.HIGHEST

_NT = (((1,), (1,)), ((), ()))
_TN = (((0,), (0,)), ((), ()))


def _const_spec(shape):
    nd = len(shape)
    return pl.BlockSpec(shape, lambda *_: (0,) * nd, pipeline_mode=pl.Buffered(1))


def _cparams(n_grid):
    return pltpu.CompilerParams(dimension_semantics=("arbitrary",) * n_grid,
                                vmem_limit_bytes=VMEM_LIMIT)


def _sigmoid(x):
    return 1.0 / (1.0 + jnp.exp(-x))


def _silu(x):
    return x * _sigmoid(x)


def _softplus(x):
    return jnp.maximum(x, 0.0) + jnp.log1p(jnp.exp(-jnp.abs(x)))


def _split_dot(a, b_bf16):
    hi = a.astype(BF16)
    lo = (a - hi.astype(F32)).astype(BF16)
    return (jnp.dot(hi, b_bf16, preferred_element_type=F32)
            + jnp.dot(lo, b_bf16, preferred_element_type=F32))


def _dot(a, b):
    if a.ndim == 3:
        return lax.dot_general(a, b, (((2,), (1,)), ((0,), (0,))), preferred_element_type=F32)
    return jnp.dot(a, b, preferred_element_type=F32)


def _mm3(a, b):
    ah = a.astype(BF16)
    al = (a - ah.astype(F32)).astype(BF16)
    bh = b.astype(BF16)
    bl = (b - bh.astype(F32)).astype(BF16)
    return _dot(ah, bh) + _dot(ah, bl) + _dot(al, bh)


def _group_sumsq(y, gsum_bf16):
    return jnp.dot((y * y).astype(BF16), gsum_bf16, preferred_element_type=F32)


def _mod_norm(x, gain, sc, sh):
    ms = jnp.mean(x * x, axis=-1, keepdims=True)
    return (x * lax.rsqrt(ms + EPS) * gain) * (1.0 + sc) + sh


def _causal_conv(xp_ref, cur, w_ref, halo_ref, first, *, width, halo, stride, rows):
    @pl.when(first)
    def _():
        xp_ref[0:halo, :] = halo_ref[0]

    xp_ref[halo:halo + rows, :] = cur
    off = halo - (width - 1) * stride
    y = xp_ref[off:off + rows, :] * w_ref[0:1, :]
    for i in range(1, width):
        off = halo - (width - 1 - i) * stride
        y = y + xp_ref[off:off + rows, :] * w_ref[i:i + 1, :]
    tail = xp_ref[rows:rows + halo, :]
    xp_ref[0:halo, :] = tail
    return y, tail


def _ada_kernel(c_ref, w_ref, b_ref, o_ref):
    c = _silu(c_ref[...]).astype(BF16)
    o_ref[0] = jnp.dot(c, w_ref[0].astype(BF16), preferred_element_type=F32) + b_ref[0]


def _ada_call(c_all, w_ada, b_ada):
    depth, d_model, n_mod = w_ada.shape
    rows = c_all.shape[0]
    bn = 1536
    return pl.pallas_call(
        _ada_kernel,
        grid=(depth, n_mod // bn),
        in_specs=[pl.BlockSpec((rows, d_model), lambda l, j: (0, 0)),
                  pl.BlockSpec((1, d_model, bn), lambda l, j: (l, 0, j)),
                  pl.BlockSpec((1, 1, bn), lambda l, j: (l, 0, j))],
        out_specs=pl.BlockSpec((1, rows, bn), lambda l, j: (l, 0, j)),
        out_shape=jax.ShapeDtypeStruct((depth, rows, n_mod), F32),
        compiler_params=_cparams(2),
    )(c_all, w_ada, b_ada.reshape(depth, 1, n_mod))


def _in_kernel(*refs, prompt, rows, tiles_per_seq, stride, halo_b, halo_c):
    (x_ref, sc_ref, sh_ref, g1_ref, wa_ref, wb_ref, wc_ref, wba_ref, wbat_ref, qn_ref, kn_ref,
     cbw_ref, cdw_ref, alog_ref, dtb_ref, alogc_ref, dtbc_ref, gsum_ref, hb_ref, hc_ref) = refs[:20]
    outs = refs[20:-2]
    xpb_ref, xpc_ref = refs[-2:]
    if prompt:
        (q_ref, k_ref, v_ref, yb_ref, tailb_ref, qc_ref, kc_ref, vc_ref, og_ref, beta_ref,
         gcol_ref, tailc_ref, kbf_ref, vt_ref, kmean_ref, grow_ref) = outs
    else:
        (q_ref, k_ref, v_ref, yb_ref, tailb_ref, qc_ref, kc_ref, vc_ref, og_ref, beta_ref,
         gcol_ref, tailc_ref) = outs

    first = (pl.program_id(0) % tiles_per_seq) == 0
    gsum = gsum_ref[...]
    h = _mod_norm(x_ref[...], g1_ref[...], sc_ref[0], sh_ref[0])
    hb16 = h.astype(BF16)

    za = jnp.dot(hb16, wa_ref[...], preferred_element_type=F32)
    qa, ka, va = za[:, :W_BR], za[:, W_BR:2 * W_BR], za[:, 2 * W_BR:]
    q = (qa * lax.rsqrt(_group_sumsq(qa, gsum) * (1.0 / D_HEAD) + EPS) * qn_ref[...]) * (D_HEAD ** -0.5)
    k = ka * lax.rsqrt(_group_sumsq(ka, gsum) * (1.0 / D_HEAD) + EPS) * kn_ref[...]
    q_ref[...] = q
    k_ref[...] = k
    v_ref[...] = va
    if prompt:
        kbf_ref[0] = k.astype(BF16)
        vt_ref[0] = va.T.astype(BF16)
        kmean_ref[0] = jnp.sum(k, axis=0, keepdims=True) * (1.0 / MOBA_BLOCK)

    zb = jnp.dot(hb16, wb_ref[...], preferred_element_type=F32)
    hb, bg, cg = zb[:, :W_BR], zb[:, W_BR:2 * W_BR], zb[:, 2 * W_BR:]
    uc, tail_b = _causal_conv(xpb_ref, cg * hb, cbw_ref, hb_ref, first,
                              width=CONV_B, halo=halo_b, stride=stride, rows=rows)
    yb_ref[...] = bg * uc
    tailb_ref[0] = tail_b

    zc = jnp.dot(hb16, wc_ref[...], preferred_element_type=F32)
    qkv, tail_c = _causal_conv(xpc_ref, zc[:, :3 * W_BR], cdw_ref, hc_ref, first,
                               width=CONV_C, halo=halo_c, stride=stride, rows=rows)
    tailc_ref[0] = tail_c
    og_ref[...] = zc[:, 3 * W_BR:]
    qkv = _silu(qkv)
    qc, kc, vc = qkv[:, :W_BR], qkv[:, W_BR:2 * W_BR], qkv[:, 2 * W_BR:]
    qc_ref[...] = qc * lax.rsqrt(_group_sumsq(qc, gsum) + EPS)
    kc_ref[...] = kc * lax.rsqrt(_group_sumsq(kc, gsum) + EPS)
    vc_ref[...] = vc
    zba = jnp.dot(hb16, wba_ref[...], preferred_element_type=F32)
    beta_ref[...] = _sigmoid(zba[:, :N_HEAD])
    gcol_ref[...] = -jnp.exp(alog_ref[...]) * _softplus(zba[:, N_HEAD:] + dtb_ref[...])
    if prompt:
        zbat = lax.dot_general(wbat_ref[...], hb16, _NT, preferred_element_type=F32)
        g_row = -jnp.exp(alogc_ref[...]) * _softplus(zbat[N_HEAD:, :] + dtbc_ref[...])
        for j in range(rows // DN_CHUNK):
            grow_ref[j] = g_row[:, j * DN_CHUNK:(j + 1) * DN_CHUNK]


def _in_call(x, sc, sh, w, *, prompt, n_seq, halo_b_in, halo_c_in):
    n_tok, d_model = x.shape
    if prompt:
        rows, stride = TM, 1
        n_tiles = n_tok // rows
        tiles_per_seq = n_tiles // n_seq
        halo_b = halo_c = SUBLANE
        mod_spec = pl.BlockSpec((1, 1, d_model), lambda t: (t // tiles_per_seq, 0, 0))
    else:
        rows, stride = n_tok, n_seq
        n_tiles, tiles_per_seq = 1, 1
        halo_b, halo_c = (CONV_B - 1) * n_seq, (CONV_C - 1) * n_seq
        mod_spec = pl.BlockSpec((1, rows, d_model), lambda t: (0, 0, 0))
    n_grp = n_tiles // tiles_per_seq

    def tok(width):
        return pl.BlockSpec((rows, width), lambda t: (t, 0))

    def per_seq(r, width):
        return pl.BlockSpec((1, r, width), lambda t: (t // tiles_per_seq, 0, 0))

    in_specs = [tok(d_model), mod_spec, mod_spec, _const_spec((1, d_model)),
                _const_spec(w['wa'].shape), _const_spec(w['wb'].shape), _const_spec(w['wc'].shape),
                _const_spec(w['wba'].shape), _const_spec(w['wbat'].shape),
                _const_spec((1, W_BR)), _const_spec((1, W_BR)),
                _const_spec((CONV_B, W_BR)), _const_spec((CONV_C, 3 * W_BR)),
                _const_spec((1, N_HEAD)), _const_spec((1, N_HEAD)),
                _const_spec((N_HEAD, 1)), _const_spec((N_HEAD, 1)),
                _const_spec((W_BR, W_BR)),
                per_seq(halo_b, W_BR), per_seq(halo_c, 3 * W_BR)]
    out_specs = [tok(W_BR), tok(W_BR), tok(W_BR), tok(W_BR), per_seq(halo_b, W_BR),
                 tok(W_BR), tok(W_BR), tok(W_BR), tok(W_BR), tok(N_HEAD), tok(N_HEAD),
                 per_seq(halo_c, 3 * W_BR)]
    out_shape = [jax.ShapeDtypeStruct((n_tok, W_BR), F32)] * 4 + [
        jax.ShapeDtypeStruct((n_grp, halo_b, W_BR), F32)] + [
        jax.ShapeDtypeStruct((n_tok, W_BR), F32)] * 4 + [
        jax.ShapeDtypeStruct((n_tok, N_HEAD), F32)] * 2 + [
        jax.ShapeDtypeStruct((n_grp, halo_c, 3 * W_BR), F32)]
    if prompt:
        cpt = rows // DN_CHUNK
        out_specs += [pl.BlockSpec((1, rows, W_BR), lambda t: (t, 0, 0)),
                      pl.BlockSpec((1, W_BR, rows), lambda t: (t, 0, 0)),
                      pl.BlockSpec((1, 1, W_BR), lambda t: (t, 0, 0)),
                      pl.BlockSpec((cpt, N_HEAD, DN_CHUNK), lambda t: (t, 0, 0))]
        out_shape += [jax.ShapeDtypeStruct((n_tiles, rows, W_BR), BF16),
                      jax.ShapeDtypeStruct((n_tiles, W_BR, rows), BF16),
                      jax.ShapeDtypeStruct((n_tiles, 1, W_BR), F32),
                      jax.ShapeDtypeStruct((n_tiles * cpt, N_HEAD, DN_CHUNK), F32)]
    kern = functools.partial(_in_kernel, prompt=prompt, rows=rows, tiles_per_seq=tiles_per_seq,
                             stride=stride, halo_b=halo_b, halo_c=halo_c)
    return pl.pallas_call(
        kern, grid=(n_tiles,), in_specs=in_specs, out_specs=out_specs, out_shape=out_shape,
        scratch_shapes=[pltpu.VMEM((halo_b + rows, W_BR), F32),
                        pltpu.VMEM((halo_c + rows, 3 * W_BR), F32)],
        compiler_params=_cparams(1),
    )(x, sc, sh, w['norm1_g'], w['wa'], w['wb'], w['wc'], w['wba'], w['wbat'], w['qn_g'], w['kn_g'],
      w['conv_b_w'], w['dn_conv_w'], w['a_log_row'], w['dt_bias_row'], w['a_log_col'],
      w['dt_bias_col'], w['gsum'], halo_b_in, halo_c_in)


def _gate_scores(a, b):
    return lax.dot_general(a.astype(BF16), b.astype(BF16), _NT, preferred_element_type=F32)


def _top3_mask_cols(gates, own, n_blk):
    blk = lax.broadcasted_iota(jnp.int32, gates.shape, 0)
    g = jnp.where(blk < own, gates, NEG)
    mask = jnp.full(gates.shape, NEG, F32)
    for _ in range(MOBA_TOPK):
        mx = jnp.max(g, axis=0, keepdims=True)
        idx = jnp.min(jnp.where(g == mx, blk, n_blk), axis=0, keepdims=True)
        hit = blk == idx
        mask = jnp.where(hit & (idx < own), 0.0, mask)
        g = jnp.where(hit, -jnp.inf, g)
    return mask


def _attn_prompt_kernel(q_ref, k_ref, vt_ref, kmean_ref, bias_ref, o_ref, sel_ref, s_ref, p_ref,
                        *, n_blk, n_dist):
    qt = pl.program_id(2)
    qf = q_ref[0]
    lane = lax.broadcasted_iota(jnp.int32, qf.shape, 1)
    ones = jnp.ones((16, MOBA_BLOCK), BF16)
    qbs = []
    for hh in range(2):
        qh = jnp.where(lane // D_HEAD == hh, qf, 0.0)
        sel_ref[hh] = _top3_mask_cols(_gate_scores(kmean_ref[0], qh), qt, n_blk)
        qbs.append(qh.astype(BF16))

    def lhs_v(hh, n):
        return jnp.concatenate([vt_ref[0, n, hh * D_HEAD:(hh + 1) * D_HEAD, :], ones], axis=0)

    def scores(hh, n):
        return lax.dot_general(k_ref[0, n], qbs[hh], _NT, preferred_element_type=F32)

    def masked_scores(hh, n):
        d = jnp.clip(qt - n, 0, n_dist - 1)
        return scores(hh, n) + bias_ref[hh, d] + sel_ref[hh, pl.ds(n, 1), :]

    carry = []
    for hh in range(2):
        s = scores(hh, qt) + bias_ref[hh, 0]
        m = jnp.max(s, axis=0, keepdims=True)
        p_ref[1, hh] = jnp.exp(s - m).astype(BF16)
        carry.append((m, jnp.ones_like(m), jnp.zeros((D_HEAD + 16, TM), F32)))
        s_ref[0, hh] = masked_scores(hh, 0)
    carry = tuple(carry)

    def pending_pv(hh, n, p_slot):
        v_blk = jnp.where(n == 0, qt, n - 1)
        return jnp.dot(lhs_v(hh, v_blk), p_ref[p_slot, hh], preferred_element_type=F32)

    def step(n, carry, slot):
        nxt = [masked_scores(hh, jnp.minimum(n + 1, n_blk - 1)) for hh in range(2)]
        pv = [pending_pv(hh, n, 1 - slot) for hh in range(2)]
        out = []
        for hh in range(2):
            m, alpha, acc = carry[hh]
            s = s_ref[slot, hh]
            m_new = jnp.maximum(m, jnp.max(s, axis=0, keepdims=True))
            p_ref[slot, hh] = jnp.exp(s - m_new).astype(BF16)
            out.append((m_new, jnp.exp(m - m_new), alpha * acc + pv[hh]))
        for hh in range(2):
            s_ref[1 - slot, hh] = nxt[hh]
        return tuple(out)

    n_pairs = (qt + 1) // 2
    carry = lax.fori_loop(0, n_pairs, lambda i, c: step(2 * i + 1, step(2 * i, c, 0), 1), carry)
    outs = []
    for hh in range(2):
        _, alpha, acc = carry[hh]
        acc = alpha * acc + pending_pv(hh, 2 * n_pairs, 1)
        outs.append(acc[:D_HEAD] / acc[D_HEAD:D_HEAD + 1])
    o_ref[0] = jnp.concatenate(outs, axis=0).T


def _attn_prompt_call(q, k_bf, vt_bf, kmean, bias_tab, n_seq):
    n_tok = q.shape[0]
    t_len = n_tok // n_seq
    n_blk = t_len // MOBA_BLOCK
    n_dist = bias_tab.shape[1]
    q3 = q.reshape(n_seq, t_len, W_BR)
    k4 = k_bf.reshape(n_seq, n_blk, MOBA_BLOCK, W_BR)
    vt4 = vt_bf.reshape(n_seq, n_blk, W_BR, MOBA_BLOCK)
    km3 = kmean.reshape(n_seq, n_blk, W_BR)
    kern = functools.partial(_attn_prompt_kernel, n_blk=n_blk, n_dist=n_dist)
    out = pl.pallas_call(
        kern,
        grid=(n_seq, N_HEAD // 2, n_blk),
        in_specs=[pl.BlockSpec((1, TM, 128), lambda b, hp, t: (b, t, hp)),
                  pl.BlockSpec((1, n_blk, MOBA_BLOCK, 128), lambda b, hp, t: (b, 0, 0, hp)),
                  pl.BlockSpec((1, n_blk, 128, MOBA_BLOCK), lambda b, hp, t: (b, 0, hp, 0)),
                  pl.BlockSpec((1, n_blk, 128), lambda b, hp, t: (b, 0, hp)),
                  pl.BlockSpec((2, n_dist, MOBA_BLOCK, MOBA_BLOCK), lambda b, hp, t: (hp, 0, 0, 0))],
        out_specs=pl.BlockSpec((1, TM, 128), lambda b, hp, t: (b, t, hp)),
        out_shape=jax.ShapeDtypeStruct((n_seq, t_len, W_BR), F32),
        scratch_shapes=[pltpu.VMEM((2, n_blk, TM), F32),
                        pltpu.VMEM((2, 2, MOBA_BLOCK, TM), F32),
                        pltpu.VMEM((2, 2, MOBA_BLOCK, TM), BF16)],
        compiler_params=_cparams(3),
    )(q3, k4, vt4, km3, bias_tab)
    return out.reshape(n_tok, W_BR)


def _stack_heads(x):
    t_len = x.shape[0]
    tiled = jnp.concatenate([x] * N_HEAD, axis=0)
    row = lax.broadcasted_iota(jnp.int32, tiled.shape, 0)
    lane = lax.broadcasted_iota(jnp.int32, tiled.shape, 1)
    return jnp.where(row // t_len == lane // D_HEAD, tiled, 0.0)


def _unstack_heads(y, t_len):
    row = lax.broadcasted_iota(jnp.int32, y.shape, 0)
    lane = lax.broadcasted_iota(jnp.int32, y.shape, 1)
    y = jnp.where(row // t_len == lane // D_HEAD, y, 0.0)
    out = y[0:t_len]
    for h in range(1, N_HEAD):
        out = out + y[h * t_len:(h + 1) * t_len]
    return out


_BQK = (((2,), (2,)), ((0,), (0,)))
_BPV = (((2,), (1,)), ((0,), (0,)))


def _split_heads(x):
    return jnp.stack([x[:, h * D_HEAD:(h + 1) * D_HEAD] for h in range(N_HEAD)], axis=0)


SAMPLE_BLOCKS_PER_STEP = 8
PAGES_PER_BLOCK = 2


def _attn_sample_kernel(pt_ref, q_ref, kn_ref, vn_ref, *refs, n_blk, t_len, page):
    del pt_ref
    n_pg = SAMPLE_BLOCKS_PER_STEP * PAGES_PER_BLOCK
    ck_refs, cv_refs = refs[:n_pg], refs[n_pg:2 * n_pg]
    bias_ref, bown_ref, o_ref, qbd_ref, kmean_ref, m_ref, l_ref, oacc_ref = refs[2 * n_pg:]
    step = pl.program_id(1)
    n_row = N_HEAD * t_len

    @pl.when(step == 0)
    def _():
        qbd_ref[...] = _stack_heads(q_ref[0]).astype(BF16)
        m_ref[...] = jnp.zeros(m_ref.shape, F32)
        l_ref[...] = jnp.zeros(l_ref.shape, F32)
        kmean_ref[...] = jnp.zeros(kmean_ref.shape, F32)

    def own_heads(wide):
        return jnp.concatenate([wide[h * t_len:(h + 1) * t_len, h * D_HEAD:(h + 1) * D_HEAD]
                                for h in range(N_HEAD)], axis=0)

    qbd = qbd_ref[...]
    col = lax.broadcasted_iota(jnp.int32, m_ref.shape, 1)
    kcol = lax.broadcasted_iota(jnp.int32, kmean_ref.shape, 2)
    m_all, l_all, kmean_all = m_ref[...], l_ref[...], kmean_ref[...]
    for j in range(SAMPLE_BLOCKS_PER_STEP):
        n = step * SAMPLE_BLOCKS_PER_STEP + j
        s_pg, ksum = [], None
        for i in range(PAGES_PER_BLOCK):
            kt = ck_refs[PAGES_PER_BLOCK * j + i][0, 0]
            part = jnp.sum(kt, axis=2, keepdims=True)
            ksum = part if ksum is None else ksum + part
            s_pg.append(jnp.dot(qbd, kt.reshape(W_BR, page).astype(BF16), preferred_element_type=F32))
        kmean_all = jnp.where(kcol == n, ksum * (1.0 / MOBA_BLOCK), kmean_all)
        s = jnp.concatenate(s_pg, axis=1) + bias_ref[n]
        m = jnp.max(s, axis=1, keepdims=True)
        p = jnp.exp(s - m)
        l = jnp.sum(p, axis=1, keepdims=True)
        pb = p.astype(BF16)
        pv = None
        for i in range(PAGES_PER_BLOCK):
            vt = cv_refs[PAGES_PER_BLOCK * j + i][0, 0].reshape(W_BR, page).astype(BF16)
            part = lax.dot_general(pb[:, i * page:(i + 1) * page], vt, _NT,
                                   preferred_element_type=F32)
            pv = part if pv is None else pv + part
        oacc_ref[n] = own_heads(pv)
        m_all = jnp.where(col == n, m, m_all)
        l_all = jnp.where(col == n, l, l_all)
    m_ref[...] = m_all
    l_ref[...] = l_all
    kmean_ref[...] = kmean_all

    @pl.when(step == pl.num_programs(1) - 1)
    def _():
        gates = jnp.dot(qbd, kmean_all.reshape(W_BR, 128).astype(BF16),
                        preferred_element_type=F32)[:, 0:n_blk]
        blk = lax.broadcasted_iota(jnp.int32, gates.shape, 1)
        g = gates
        sel = jnp.zeros(gates.shape, jnp.bool_)
        for _ in range(MOBA_TOPK):
            mx = jnp.max(g, axis=1, keepdims=True)
            idx = jnp.min(jnp.where(g == mx, blk, n_blk), axis=1, keepdims=True)
            hit = blk == idx
            sel = sel | hit
            g = jnp.where(hit, -jnp.inf, g)
        m_all = m_ref[:, 0:n_blk]
        l_all = l_ref[:, 0:n_blk]

        pad = jnp.zeros((128 - t_len, W_BR), F32)
        kown = jnp.concatenate([kn_ref[0], pad], axis=0).astype(BF16)
        vown = jnp.concatenate([vn_ref[0], pad], axis=0).astype(BF16)
        s_own = lax.dot_general(qbd, kown, _NT, preferred_element_type=F32) + bown_ref[...]
        m_own = jnp.max(s_own, axis=1, keepdims=True)
        m_tot = jnp.maximum(m_own, jnp.max(jnp.where(sel, m_all, -jnp.inf), axis=1, keepdims=True))
        p_own = jnp.exp(s_own - m_tot)
        wgt = jnp.where(sel, jnp.exp(m_all - m_tot), 0.0)
        denom = jnp.sum(p_own, axis=1, keepdims=True) + jnp.sum(wgt * l_all, axis=1, keepdims=True)
        acc = own_heads(jnp.dot(p_own.astype(BF16), vown, preferred_element_type=F32))
        for i in range(n_blk):
            acc = acc + wgt[:, i:i + 1] * oacc_ref[i]
        out = acc / denom
        o_ref[0] = jnp.concatenate([out[h * t_len:(h + 1) * t_len] for h in range(N_HEAD)], axis=1)


def _attn_sample_call(page_table, q, k_new, v_new, cache_k, cache_v, layer, bias_tab, bias_own):
    n_seq, t_len, _ = q.shape
    n_pages = page_table.shape[1]
    page = cache_k.shape[4]
    n_blk = n_pages * page // MOBA_BLOCK
    n_row = N_HEAD * t_len
    tok = pl.BlockSpec((1, t_len, W_BR), lambda b, n, pt: (b, 0, 0))

    n_pg = SAMPLE_BLOCKS_PER_STEP * PAGES_PER_BLOCK
    assert page * PAGES_PER_BLOCK == MOBA_BLOCK and n_blk % SAMPLE_BLOCKS_PER_STEP == 0 and n_blk <= 128

    def page_spec(j):
        return pl.BlockSpec((1, 1, N_HEAD, D_HEAD, page),
                            lambda b, n, pt: (layer, pt[b, n_pg * n + j], 0, 0, 0))

    pages = [page_spec(j) for j in range(n_pg)]
    kern = functools.partial(_attn_sample_kernel, n_blk=n_blk, t_len=t_len, page=page)
    return pl.pallas_call(
        kern,
        grid_spec=pltpu.PrefetchScalarGridSpec(
            num_scalar_prefetch=1,
            grid=(n_seq, n_blk // SAMPLE_BLOCKS_PER_STEP),
            in_specs=[tok, tok, tok] + pages + pages + [
                      pl.BlockSpec(bias_tab.shape, lambda b, n, pt: (0, 0, 0)),
                      pl.BlockSpec(bias_own.shape, lambda b, n, pt: (0, 0))],
            out_specs=tok,
            scratch_shapes=[pltpu.VMEM((n_row, W_BR), BF16),
                            pltpu.VMEM((N_HEAD, D_HEAD, 128), F32), pltpu.VMEM((n_row, 128), F32),
                            pltpu.VMEM((n_row, 128), F32), pltpu.VMEM((n_blk, n_row, D_HEAD), F32)]),
        out_shape=jax.ShapeDtypeStruct((n_seq, t_len, W_BR), F32),
        compiler_params=_cparams(2),
    )(page_table, q, k_new, v_new, *([cache_k] * n_pg), *([cache_v] * n_pg), bias_tab, bias_own)


SOLVE_BASE = 4

def _unit_lower_inverse(a, size):
    n = a.shape[-1]
    ri = lax.broadcasted_iota(jnp.int32, (n, n), 0)
    ci = lax.broadcasted_iota(jnp.int32, (n, n), 1)
    eye = (ri == ci).astype(F32)

    def same_block(s):
        return (ri // s) == (ci // s)

    d = jnp.where(same_block(SOLVE_BASE), a, 0.0)
    inv = _dot(eye - d, eye + _dot(d, d))
    s = SOLVE_BASE
    while s < size:
        off = jnp.where(same_block(2 * s) & jnp.logical_not(same_block(s)), a, 0.0)
        inv = inv - _dot(_dot(inv, off), inv)
        s *= 2
    return inv


def _gdn_prompt_kernel(q_ref, k_ref, v_ref, beta_ref, gcol_ref, grow_ref, o_ref, s_ref, state_ref,
                       *, chunks):
    @pl.when(pl.program_id(1) == 0)
    def _():
        state_ref[...] = jnp.zeros(state_ref.shape, F32)

    ri = lax.broadcasted_iota(jnp.int32, (DN_CHUNK, DN_CHUNK), 0)
    ci = lax.broadcasted_iota(jnp.int32, (DN_CHUNK, DN_CHUNK), 1)
    tril = ri >= ci
    strict = ri > ci
    lower = tril.astype(F32)

    def split(x, r0):
        return [x[r0:r0 + DN_CHUNK, h * D_HEAD:(h + 1) * D_HEAD] for h in range(N_HEAD)]

    qa = q_ref[0] * (D_HEAD ** -0.5)
    ka, va, beta, gcol = k_ref[0], v_ref[0], beta_ref[0], gcol_ref[0]
    q, k, v, kt, gc, gr, b = [], [], [], [], [], [], []
    for c in range(chunks):
        r0 = c * DN_CHUNK
        q += split(qa, r0)
        k += split(ka, r0)
        v += split(va, r0)
        kt.append(ka[r0:r0 + DN_CHUNK].T.reshape(N_HEAD, D_HEAD, DN_CHUNK))
        gcum_col = jnp.dot(lower, gcol[r0:r0 + DN_CHUNK], precision=HIGHEST,
                           preferred_element_type=F32)
        gcum_row = lax.dot_general(grow_ref[c], lower, _NT, precision=HIGHEST,
                                   preferred_element_type=F32)
        gc += [gcum_col[:, h:h + 1] for h in range(N_HEAD)]
        gr += [gcum_row[h:h + 1, :] for h in range(N_HEAD)]
        b += [beta[r0:r0 + DN_CHUNK, h:h + 1] for h in range(N_HEAD)]
    q, k, v = jnp.stack(q, axis=0), jnp.stack(k, axis=0), jnp.stack(v, axis=0)
    kt = jnp.concatenate(kt, axis=0)
    gc, gr, b = jnp.stack(gc, axis=0), jnp.stack(gr, axis=0), jnp.stack(b, axis=0)
    decay = jnp.where(tril, jnp.exp(jnp.where(tril, gc - gr, 0.0)), 0.0)
    eg = jnp.exp(gc)
    glast = gr[:, :, DN_CHUNK - 1:DN_CHUNK]
    kb = k * b
    a = jnp.where(strict, _dot(kb, kt) * decay, 0.0)
    intra = _dot(q, kt) * decay
    x = _dot(_unit_lower_inverse(a, DN_CHUNK), jnp.concatenate([v * b, kb * eg], axis=2))
    u, w = x[:, :, :D_HEAD], x[:, :, D_HEAD:]
    q_dec = q * eg
    kt_dec = kt * jnp.exp(glast - gr)
    s_decay = jnp.exp(glast)

    s = state_ref[...]
    rows = []
    for c in range(chunks):
        sl = slice(c * N_HEAD, (c + 1) * N_HEAD)
        v_new = u[sl] - _dot(w[sl], s)
        o = _dot(q_dec[sl], s) + _dot(intra[sl], v_new)
        s = s * s_decay[sl] + _dot(kt_dec[sl], v_new)
        rows.append(jnp.concatenate([o[h] for h in range(N_HEAD)], axis=1))
    o_ref[0] = jnp.concatenate(rows, axis=0)
    state_ref[...] = s
    s_ref[0] = s


def _gdn_prompt_call(qc, kc, vc, beta, gcol, grow, n_seq):
    n_tok = qc.shape[0]
    t_len = n_tok // n_seq
    rows = TM
    chunks = rows // DN_CHUNK
    n_tiles = t_len // rows

    def tok(width):
        return pl.BlockSpec((1, rows, width), lambda b, t: (b, t, 0))

    o, s = pl.pallas_call(
        functools.partial(_gdn_prompt_kernel, chunks=chunks),
        grid=(n_seq, n_tiles),
        in_specs=[tok(W_BR), tok(W_BR), tok(W_BR), tok(N_HEAD), tok(N_HEAD),
                  pl.BlockSpec((chunks, N_HEAD, DN_CHUNK), lambda b, t: (b * n_tiles + t, 0, 0))],
        out_specs=[tok(W_BR), pl.BlockSpec((1, N_HEAD, D_HEAD, D_HEAD), lambda b, t: (b, 0, 0, 0))],
        out_shape=[jax.ShapeDtypeStruct((n_seq, t_len, W_BR), F32),
                   jax.ShapeDtypeStruct((n_seq, N_HEAD, D_HEAD, D_HEAD), F32)],
        scratch_shapes=[pltpu.VMEM((N_HEAD, D_HEAD, D_HEAD), F32)],
        compiler_params=_cparams(2),
    )(qc.reshape(n_seq, t_len, W_BR), kc.reshape(n_seq, t_len, W_BR), vc.reshape(n_seq, t_len, W_BR),
      beta.reshape(n_seq, t_len, N_HEAD), gcol.reshape(n_seq, t_len, N_HEAD), grow)
    return o.reshape(n_tok, W_BR), s


def _gdn_sample_kernel(q_ref, k_ref, v_ref, bcol_ref, gcol_ref, grow_ref, s0_ref, o_ref, s_ref,
                       *, t_len):
    n_row = N_HEAD * t_len
    ri = lax.broadcasted_iota(jnp.int32, (n_row, n_row), 0)
    ci = lax.broadcasted_iota(jnp.int32, (n_row, n_row), 1)
    same = (ri // t_len) == (ci // t_len)
    tril = same & (ri >= ci)
    strict = same & (ri > ci)
    q = _stack_heads(q_ref[0]) * (D_HEAD ** -0.5)
    k = _stack_heads(k_ref[0])
    v = _stack_heads(v_ref[0])
    b = bcol_ref[0]
    g_lane = jnp.broadcast_to(gcol_ref[0], (n_row, 128))
    gc = jnp.dot(tril.astype(F32), g_lane, precision=HIGHEST, preferred_element_type=F32)[:, 0:1]
    gl = jnp.dot(same.astype(F32), g_lane, precision=HIGHEST, preferred_element_type=F32)[:, 0:1]
    g_sub = jnp.broadcast_to(grow_ref[0], (SUBLANE, n_row))
    gr = lax.dot_general(g_sub, tril.astype(F32), _NT, precision=HIGHEST,
                         preferred_element_type=F32)[0:1, :]
    decay = jnp.where(tril, jnp.exp(jnp.where(tril, gc - gr, 0.0)), 0.0)
    eg = jnp.exp(gc)
    kb = k * b
    a = jnp.where(strict, lax.dot_general(kb, k, _NT, preferred_element_type=F32) * decay, 0.0)
    intra = lax.dot_general(q, k, _NT, preferred_element_type=F32) * decay
    vb = v * b
    vb_c = vb[:, 0:D_HEAD]
    for h in range(1, N_HEAD):
        vb_c = vb_c + vb[:, h * D_HEAD:(h + 1) * D_HEAD]
    x = _mm3(_unit_lower_inverse(a, t_len), jnp.concatenate([kb * eg, vb_c], axis=1))
    w, u = x[:, :W_BR], x[:, W_BR:]
    s0 = s0_ref[0]
    v_new = u - jnp.dot(w, s0, preferred_element_type=F32)
    o = jnp.dot(q * eg, s0, preferred_element_type=F32) + jnp.dot(intra, v_new, preferred_element_type=F32)
    er = lax.broadcasted_iota(jnp.int32, (W_BR, n_row), 0)
    ec = lax.broadcasted_iota(jnp.int32, (W_BR, n_row), 1)
    pick = ((er // D_HEAD) == (ec // t_len)) & ((ec % t_len) == t_len - 1)
    gl_state = jnp.dot(pick.astype(F32), jnp.broadcast_to(gc, (n_row, 128)), precision=HIGHEST,
                       preferred_element_type=F32)[:, 0:1]
    s_ref[0] = s0 * jnp.exp(gl_state) + lax.dot_general(
        k * jnp.exp(gl - gc), v_new, _TN, preferred_element_type=F32)
    o_ref[0] = _unstack_heads(jnp.concatenate([o] * N_HEAD, axis=1), t_len)


def _gdn_sample_call(qc, kc, vc, bcol, gcol, grow, s0):
    n_seq, t_len, _ = qc.shape
    n_row = N_HEAD * t_len
    assert t_len >= SOLVE_BASE and t_len & (t_len - 1) == 0
    tok = pl.BlockSpec((1, t_len, W_BR), lambda b: (b, 0, 0))
    colv = pl.BlockSpec((1, n_row, 1), lambda b: (b, 0, 0))
    st = pl.BlockSpec((1, W_BR, D_HEAD), lambda b: (b, 0, 0))
    return pl.pallas_call(
        functools.partial(_gdn_sample_kernel, t_len=t_len),
        grid=(n_seq,),
        in_specs=[tok, tok, tok, colv, colv, pl.BlockSpec((1, 1, n_row), lambda b: (b, 0, 0)), st],
        out_specs=[tok, st],
        out_shape=[jax.ShapeDtypeStruct((n_seq, t_len, W_BR), F32),
                   jax.ShapeDtypeStruct((n_seq, W_BR, D_HEAD), F32)],
        compiler_params=_cparams(1),
    )(qc, kc, vc, bcol, gcol, grow, s0)


def _out_kernel(x_ref, sc_ref, sh_ref, gate_ref, g1_ref, attn_ref, yb_ref, od_ref, og_ref, dng_ref,
                wmg_ref, woa_ref, wob_ref, woc_ref, wo_ref, gsum_ref, o_ref):
    x = x_ref[...]
    h = _mod_norm(x, g1_ref[...], sc_ref[0], sh_ref[0]).astype(BF16)
    mg = _sigmoid(jnp.dot(h, wmg_ref[...], preferred_element_type=F32))
    d = x.shape[1]
    ya = jnp.dot(attn_ref[...].astype(BF16), woa_ref[...], preferred_element_type=F32)
    yb = jnp.dot(yb_ref[...].astype(BF16), wob_ref[...], preferred_element_type=F32)
    od = od_ref[...]
    oc = (od * lax.rsqrt(_group_sumsq(od, gsum_ref[...]) * (1.0 / D_HEAD) + EPS) * dng_ref[...]) \
        * _silu(og_ref[...])
    yc = jnp.dot(oc.astype(BF16), woc_ref[...], preferred_element_type=F32)
    merged = mg[:, :d] * ya + mg[:, d:2 * d] * yb + mg[:, 2 * d:] * yc
    o_ref[...] = x + gate_ref[0] * jnp.dot(merged.astype(BF16), wo_ref[...], preferred_element_type=F32)


def _token_grid(n_tok, n_seq, prompt, d_model):
    if prompt:
        rows = TM
        n_tiles = n_tok // rows
        tiles_per_seq = n_tiles // n_seq
        mod_spec = pl.BlockSpec((1, 1, d_model), lambda t: (t // tiles_per_seq, 0, 0))
    else:
        rows, n_tiles, tiles_per_seq = n_tok, 1, 1
        mod_spec = pl.BlockSpec((1, rows, d_model), lambda t: (0, 0, 0))
    return rows, n_tiles, tiles_per_seq, mod_spec


def _out_call(x, sc, sh, gate, attn, yb, od, og, w, *, prompt, n_seq):
    n_tok, d_model = x.shape
    rows, n_tiles, _, mod_spec = _token_grid(n_tok, n_seq, prompt, d_model)

    def tok(width):
        return pl.BlockSpec((rows, width), lambda t: (t, 0))

    return pl.pallas_call(
        _out_kernel, grid=(n_tiles,),
        in_specs=[tok(d_model), mod_spec, mod_spec, mod_spec, _const_spec((1, d_model)),
                  tok(W_BR), tok(W_BR), tok(W_BR), tok(W_BR), _const_spec((1, W_BR)),
                  _const_spec(w['wmg'].shape), _const_spec(w['woa'].shape), _const_spec(w['wob'].shape),
                  _const_spec(w['woc'].shape), _const_spec(w['wo'].shape), _const_spec((W_BR, W_BR))],
        out_specs=tok(d_model),
        out_shape=jax.ShapeDtypeStruct((n_tok, d_model), F32),
        compiler_params=_cparams(1),
    )(x, sc, sh, gate, w['norm1_g'], attn, yb, od, og, w['dn_norm_g'], w['wmg'], w['woa'], w['wob'],
      w['woc'], w['wo'], w['gsum'])


def _ffn_kernel(x_ref, sc_ref, sh_ref, gate_ref, g2_ref, wg_ref, wu_ref, wd_ref, cw_ref, halo_ref,
                o_ref, tail_ref, xp_ref, *, rows, tiles_per_seq, stride, halo):
    first = (pl.program_id(0) % tiles_per_seq) == 0
    x = x_ref[...]
    h = _mod_norm(x, g2_ref[...], sc_ref[0], sh_ref[0]).astype(BF16)
    pre = jnp.dot(h, wg_ref[...], preferred_element_type=F32)
    hg, tail = _causal_conv(xp_ref, pre, cw_ref, halo_ref, first,
                            width=CONV_F, halo=halo, stride=stride, rows=rows)
    tail_ref[0] = tail
    act = _silu(hg) * jnp.dot(h, wu_ref[...], preferred_element_type=F32)
    o_ref[...] = x + gate_ref[0] * jnp.dot(act.astype(BF16), wd_ref[...], preferred_element_type=F32)


def _ffn_call(x, sc, sh, gate, w, halo_in, *, prompt, n_seq):
    n_tok, d_model = x.shape
    d_ff = w['wgate'].shape[1]
    rows, n_tiles, tiles_per_seq, mod_spec = _token_grid(n_tok, n_seq, prompt, d_model)
    stride = 1 if prompt else n_seq
    halo = SUBLANE if prompt else (CONV_F - 1) * n_seq
    n_grp = n_tiles // tiles_per_seq
    tok = pl.BlockSpec((rows, d_model), lambda t: (t, 0))
    per_seq = pl.BlockSpec((1, halo, d_ff), lambda t: (t // tiles_per_seq, 0, 0))
    kern = functools.partial(_ffn_kernel, rows=rows, tiles_per_seq=tiles_per_seq, stride=stride, halo=halo)
    return pl.pallas_call(
        kern, grid=(n_tiles,),
        in_specs=[tok, mod_spec, mod_spec, mod_spec, _const_spec((1, d_model)),
                  _const_spec(w['wgate'].shape), _const_spec(w['wup'].shape), _const_spec(w['wdown'].shape),
                  _const_spec((CONV_F, d_ff)), per_seq],
        out_specs=[tok, per_seq],
        out_shape=[jax.ShapeDtypeStruct((n_tok, d_model), F32),
                   jax.ShapeDtypeStruct((n_grp, halo, d_ff), F32)],
        scratch_shapes=[pltpu.VMEM((halo + rows, d_ff), F32)],
        compiler_params=_cparams(1),
    )(x, sc, sh, gate, w['norm2_g'], w['wgate'], w['wup'], w['wdown'], w['ffn_conv_w'], halo_in)


def _t5_bucket(rel):
    n = jnp.maximum(rel, 0)
    max_exact = REL_BUCKETS // 2
    nf = jnp.maximum(n, 1).astype(F32)
    large = max_exact + (jnp.log(nf / max_exact) / math.log(REL_MAX_DIST / max_exact)
                         * (REL_BUCKETS - max_exact)).astype(jnp.int32)
    large = jnp.minimum(large, REL_BUCKETS - 1)
    return jnp.where(n < max_exact, n, large)


def _bias_tables(rel_bias, n_blk_prompt, past_len, t_s):
    blk = MOBA_BLOCK

    def lookup(rel):
        onehot = jax.nn.one_hot(_t5_bucket(rel).reshape(-1), REL_BUCKETS, dtype=F32)
        tab = lax.dot_general(rel_bias, onehot, (((0,), (1,)), ((), ())), precision=HIGHEST)
        return jnp.where(rel.reshape(-1) >= 0, tab, NEG).reshape((N_HEAD,) + rel.shape)

    n_dist = min(n_blk_prompt, -(-(REL_MAX_DIST + blk - 1) // blk) + 1)
    d = jnp.arange(n_dist, dtype=jnp.int32)[:, None, None]
    ki = jnp.arange(blk, dtype=jnp.int32)[None, :, None]
    qj = jnp.arange(blk, dtype=jnp.int32)[None, None, :]
    tab_p = lookup(d * blk + qj - ki)
    n_blk_s = past_len // blk
    nb = jnp.arange(n_blk_s, dtype=jnp.int32)[:, None, None]
    tq = jnp.arange(t_s, dtype=jnp.int32)[None, :, None]
    ko = jnp.arange(blk, dtype=jnp.int32)[None, None, :]
    tab_s = lookup(past_len + tq - nb * blk - ko)
    tab_s = jnp.transpose(tab_s, (1, 0, 2, 3)).reshape(n_blk_s, N_HEAD * t_s, blk)
    rel_o = jnp.arange(t_s, dtype=jnp.int32)[:, None] - jnp.arange(128, dtype=jnp.int32)[None, :]
    tab_o = lookup(rel_o).reshape(N_HEAD * t_s, 128)
    return tab_p.astype(F32), tab_s.astype(F32), tab_o.astype(F32)


def _layer_weights(l, p, gsum):
    w_in = p['w_in'][l]
    cuts = [0, 3 * W_BR, 6 * W_BR, 10 * W_BR, 10 * W_BR + 2 * N_HEAD]
    wba = w_in[:, cuts[3]:cuts[4]].astype(BF16)
    row = lambda a: a.reshape(1, -1)
    return {
        'wa': w_in[:, cuts[0]:cuts[1]].astype(BF16),
        'wb': w_in[:, cuts[1]:cuts[2]].astype(BF16),
        'wc': w_in[:, cuts[2]:cuts[3]].astype(BF16),
        'wba': wba, 'wbat': wba.T,
        'wmg': w_in[:, cuts[4]:].astype(BF16),
        'norm1_g': row(p['norm1_g'][l]), 'norm2_g': row(p['norm2_g'][l]),
        'qn_g': row(jnp.tile(p['qn_g'][l], N_HEAD)), 'kn_g': row(jnp.tile(p['kn_g'][l], N_HEAD)),
        'dn_norm_g': row(jnp.tile(p['dn_norm_g'][l], N_HEAD)),
        'conv_b_w': p['conv_b_w'][l], 'dn_conv_w': p['dn_conv_w'][l], 'ffn_conv_w': p['ffn_conv_w'][l],
        'a_log_row': row(p['dn_a_log'][l]), 'dt_bias_row': row(p['dn_dt_bias'][l]),
        'a_log_col': p['dn_a_log'][l].reshape(-1, 1), 'dt_bias_col': p['dn_dt_bias'][l].reshape(-1, 1),
        'woa': p['w_oa'][l].astype(BF16), 'wob': p['w_ob'][l].astype(BF16), 'woc': p['w_oc'][l].astype(BF16),
        'wo': p['w_o'][l].astype(BF16),
        'wgate': p['w_gate'][l].astype(BF16), 'wup': p['w_up'][l].astype(BF16),
        'wdown': p['w_down'][l].astype(BF16),
        'gsum': gsum,
    }


def kernel(x_prompt, x_sample, cache_k, cache_v, state_conv_b, state_conv_dn, state_dn, state_conv_ffn,
           page_table, c_prompt, c_sample, rel_bias, w_ada, b_ada, norm1_g, norm2_g, w_in, qn_g, kn_g,
           conv_b_w, dn_conv_w, dn_a_log, dn_dt_bias, dn_norm_g, w_oa, w_ob, w_oc, w_o, w_gate, w_up,
           ffn_conv_w, w_down):
    bp, tp, d_model = x_prompt.shape
    bs, ts, _ = x_sample.shape
    depth = w_in.shape[0]
    d_ff = w_gate.shape[2]
    n_pool, page = cache_k.shape[1], cache_k.shape[2]
    past_len = page_table.shape[1] * page
    assert tp % MOBA_BLOCK == 0 and past_len % MOBA_BLOCK == 0 and MOBA_BLOCK == 2 * page
    assert ts <= 128 and past_len // MOBA_BLOCK >= MOBA_TOPK and (bs * ts) % SUBLANE == 0
    p = dict(w_in=w_in, norm1_g=norm1_g, norm2_g=norm2_g, qn_g=qn_g, kn_g=kn_g, conv_b_w=conv_b_w,
             dn_conv_w=dn_conv_w, dn_a_log=dn_a_log, dn_dt_bias=dn_dt_bias, dn_norm_g=dn_norm_g,
             w_oa=w_oa, w_ob=w_ob, w_oc=w_oc, w_o=w_o, w_gate=w_gate, w_up=w_up,
             ffn_conv_w=ffn_conv_w, w_down=w_down)

    lane = jnp.arange(W_BR) // D_HEAD
    gsum = (lane[:, None] == lane[None, :]).astype(BF16)
    tab_p, tab_s, tab_o = _bias_tables(rel_bias, tp // MOBA_BLOCK, past_len, ts)

    n_c = bp + bs
    c_rows = -(-n_c // SUBLANE) * SUBLANE
    c_all = jnp.concatenate([c_prompt, c_sample, jnp.zeros((c_rows - n_c, d_model), F32)], axis=0)
    mods = _ada_call(c_all, w_ada, b_ada)

    ck = jnp.transpose(cache_k, (0, 1, 3, 4, 2))
    cv = jnp.transpose(cache_v, (0, 1, 3, 4, 2))

    xp = x_prompt.reshape(bp * tp, d_model)
    xs = jnp.transpose(x_sample, (1, 0, 2)).reshape(ts * bs, d_model)

    def to_tm(a):
        return jnp.transpose(a, (1, 0, 2)).reshape(1, a.shape[1] * bs, a.shape[2])

    def from_tm(a, r):
        return jnp.transpose(a.reshape(r, bs, a.shape[-1]), (1, 0, 2))

    zero_b = jnp.zeros((bp, SUBLANE, W_BR), F32)
    zero_c = jnp.zeros((bp, SUBLANE, 3 * W_BR), F32)
    zero_f = jnp.zeros((bp, SUBLANE, d_ff), F32)

    outs = [[] for _ in range(12)]
    for l in range(depth):
        w = _layer_weights(l, p, gsum)
        mod_p = [m.reshape(bp, 1, d_model) for m in jnp.split(mods[l, :bp], 6, axis=-1)]
        mod_s = [jnp.tile(m, (ts, 1)).reshape(1, ts * bs, d_model)
                 for m in jnp.split(mods[l, bp:bp + bs], 6, axis=-1)]

        sh1, sc1, g1, sh2, sc2, g2 = mod_p
        (q, k, v, yb, tail_b, qc, kc, vc, og, beta, gcol, tail_c, k_bf, vt_bf, kmean, grow) = _in_call(
            xp, sc1, sh1, w, prompt=True, n_seq=bp, halo_b_in=zero_b, halo_c_in=zero_c)
        attn = _attn_prompt_call(q, k_bf, vt_bf, kmean, tab_p, bp)
        od, s_p = _gdn_prompt_call(qc, kc, vc, beta, gcol, grow, bp)
        xp = _out_call(xp, sc1, sh1, g1, attn, yb, od, og, w, prompt=True, n_seq=bp)
        xp, tail_f = _ffn_call(xp, sc2, sh2, g2, w, zero_f, prompt=True, n_seq=bp)
        outs[0].append(k.reshape(bp, tp, N_HEAD, D_HEAD))
        outs[1].append(v.reshape(bp, tp, N_HEAD, D_HEAD))
        outs[4].append(tail_b[:, SUBLANE - (CONV_B - 1):])
        outs[6].append(tail_c[:, SUBLANE - (CONV_C - 1):])
        outs[8].append(s_p)
        outs[10].append(tail_f[:, SUBLANE - (CONV_F - 1):])

        sh1, sc1, g1, sh2, sc2, g2 = mod_s
        (q, k, v, yb, tail_b, qc, kc, vc, og, beta, gcol, tail_c) = _in_call(
            xs, sc1, sh1, w, prompt=False, n_seq=bs,
            halo_b_in=to_tm(state_conv_b[l]), halo_c_in=to_tm(state_conv_dn[l]))
        q_b, k_b, v_b = from_tm(q, ts), from_tm(k, ts), from_tm(v, ts)
        attn = _attn_sample_call(page_table, q_b, k_b, v_b, ck, cv, l, tab_s, tab_o)
        beta_ht = jnp.transpose(beta.reshape(ts, bs, N_HEAD), (1, 2, 0)).reshape(bs, N_HEAD * ts)
        g_ht = jnp.transpose(gcol.reshape(ts, bs, N_HEAD), (1, 2, 0)).reshape(bs, N_HEAD * ts)
        od, s_s = _gdn_sample_call(from_tm(qc, ts), from_tm(kc, ts), from_tm(vc, ts),
                                   beta_ht[:, :, None], g_ht[:, :, None], g_ht[:, None, :],
                                   state_dn[l].reshape(bs, W_BR, D_HEAD))
        attn_tm = jnp.transpose(attn, (1, 0, 2)).reshape(ts * bs, W_BR)
        od_tm = jnp.transpose(od, (1, 0, 2)).reshape(ts * bs, W_BR)
        xs = _out_call(xs, sc1, sh1, g1, attn_tm, yb, od_tm, og, w, prompt=False, n_seq=bs)
        xs, tail_f = _ffn_call(xs, sc2, sh2, g2, w, to_tm(state_conv_ffn[l]), prompt=False, n_seq=bs)
        outs[2].append(k_b.reshape(bs, ts, N_HEAD, D_HEAD))
        outs[3].append(v_b.reshape(bs, ts, N_HEAD, D_HEAD))
        outs[5].append(from_tm(tail_b[0], CONV_B - 1))
        outs[7].append(from_tm(tail_c[0], CONV_C - 1))
        outs[9].append(s_s.reshape(bs, N_HEAD, D_HEAD, D_HEAD))
        outs[11].append(from_tm(tail_f[0], CONV_F - 1))

    y_p = xp.reshape(bp, tp, d_model)
    y_s = from_tm(xs, ts)
    return (y_p, y_s) + tuple(jnp.stack(o) for o in outs)
```
